```python
import math
import jax
import jax.numpy as jnp
from jax import lax
import numpy as np

D_MODEL = 2048
BATCH = 2
SEQ = 16384
DEPTH = 2
DEC_BATCH = 4
DEC_SEQ = 2048
PAST_LEN = 128

N_HEADS = 16
Q_RANK = 512
KV_RANK = 512
QK_NOPE = 128
QK_ROPE = 64
V_DIM = 128
ROPE_THETA = 10000.0
Q_BLOCK = 128
ATTN_SCALE = 1.0 / math.sqrt(QK_NOPE + QK_ROPE)
HY_ORDER = 2
HY_EMB = 33
HY_BANDS = (HY_EMB - 1) // 2
HY_HIDDEN = 64
HY_FAST_DECAY = 0.3
HY_SLOW_DECAY = 1.5
HY_TARGET = 1e-2
N_GROUPS = 8
EXPERTS_PER_GROUP = 8
N_EXPERTS = N_GROUPS * EXPERTS_PER_GROUP
TOP_K = 2
D_EXPERT = 512
ROW_BLOCK = 128
NORM_EPS = 1e-6
N_MLA = (DEPTH + 1) // 2
N_HYENA = DEPTH // 2

kernel_name = 'hybrid_mla_hyena_hmoe_encoder'


def rms_norm(x, g):
    xf = x.astype(jnp.float32)
    y = xf * lax.rsqrt(jnp.mean(xf * xf, axis=-1, keepdims=True) + NORM_EPS)
    return (y * g.astype(jnp.float32)).astype(x.dtype)


def rope_tables(L):
    inv = 1.0 / (ROPE_THETA ** (jnp.arange(0, QK_ROPE, 2, dtype=jnp.float32) / QK_ROPE))
    ang = jnp.arange(L, dtype=jnp.float32)[:, None] * inv[None, :]
    ang = jnp.concatenate([ang, ang], axis=-1)
    return jnp.cos(ang), jnp.sin(ang)


def apply_rope(x, cos, sin):
    xf = x.astype(jnp.float32)
    half = QK_ROPE // 2
    rot = jnp.concatenate([-xf[..., half:], xf[..., :half]], axis=-1)
    return (xf * cos + rot * sin).astype(x.dtype)


def blocked_attention(qn, qr, kn, kr, v):
    b, L, h, _ = qn.shape
    nq = L // Q_BLOCK
    qn_b = qn.reshape(b, nq, Q_BLOCK, h, QK_NOPE).transpose(1, 0, 2, 3, 4)
    qr_b = qr.reshape(b, nq, Q_BLOCK, h, QK_ROPE).transpose(1, 0, 2, 3, 4)

    def one_block(args):
        qnb, qrb = args
        s = (jnp.einsum('bqhd,bkhd->bhqk', qnb, kn).astype(jnp.float32)
             + jnp.einsum('bqhr,bkr->bhqk', qrb, kr).astype(jnp.float32))
        p = jax.nn.softmax(s * ATTN_SCALE, axis=-1)
        return jnp.einsum('bhqk,bkhd->bqhd', p.astype(v.dtype), v)

    o = lax.map(one_block, (qn_b, qr_b))
    return o.transpose(1, 0, 2, 3, 4).reshape(b, L, h, V_DIM)


def mla_mixer(h, w_in, q_norm, w_q_up, kv_norm, w_kv_up, w_out):
    b, L, _ = h.shape
    c = h @ w_in
    cq = c[..., :Q_RANK]
    ckv = c[..., Q_RANK:Q_RANK + KV_RANK]
    kr = c[..., Q_RANK + KV_RANK:]
    q = (rms_norm(cq, q_norm) @ w_q_up).reshape(b, L, N_HEADS, QK_NOPE + QK_ROPE)
    kv = (rms_norm(ckv, kv_norm) @ w_kv_up).reshape(b, L, N_HEADS, QK_NOPE + V_DIM)
    cos, sin = rope_tables(L)
    qn = q[..., :QK_NOPE]
    qr = apply_rope(q[..., QK_NOPE:], cos[:, None, :], sin[:, None, :])
    kn = kv[..., :QK_NOPE]
    v = kv[..., QK_NOPE:]
    kr = apply_rope(kr, cos, sin)
    o = blocked_attention(qn, qr, kn, kr, v)
    return o.reshape(b, L, N_HEADS * V_DIM) @ w_out


def hyena_filter_spectrum(L, f_w1, f_b1, f_fr1, f_w2, f_b2, f_fr2, f_w3, f_b3):
    f32 = jnp.float32
    t = jnp.linspace(0.0, 1.0, L, dtype=f32)[:, None]
    w = (2.0 * math.pi / L) * jnp.arange(L, dtype=f32)[:, None]
    bands = jnp.linspace(1e-4, HY_BANDS - 1, HY_BANDS, dtype=f32)[None, :]
    feats = jnp.concatenate([t, jnp.cos(bands * w), -jnp.sin(bands * w)], axis=-1)
    a = jnp.sin(f_fr1.astype(f32) * (feats @ f_w1.astype(f32) + f_b1.astype(f32)))
    a = jnp.sin(f_fr2.astype(f32) * (a @ f_w2.astype(f32) + f_b2.astype(f32)))
    filt = (a @ f_w3.astype(f32) + f_b3.astype(f32)).reshape(L, HY_ORDER, 2, D_MODEL)
    max_decay = math.log(HY_TARGET) / HY_FAST_DECAY
    min_decay = math.log(HY_TARGET) / HY_SLOW_DECAY
    deltas = jnp.abs(jnp.linspace(min_decay, max_decay, D_MODEL, dtype=f32))
    filt = filt * jnp.exp(-t * deltas[None, :])[:, None, None, :]
    filt = filt / (jnp.sum(jnp.abs(filt), axis=0, keepdims=True) + 1e-6)
    fwd = filt[:, :, 0]
    bwd = filt[:, :, 1]
    taps = jnp.concatenate([fwd, jnp.zeros((1, HY_ORDER, D_MODEL), f32), bwd[:0:-1]], axis=0)
    return jnp.fft.rfft(taps, n=2 * L, axis=0)


def long_conv(v, spec, skip):
    L = v.shape[1]
    vf = v.astype(jnp.float32)
    y = jnp.fft.irfft(jnp.fft.rfft(vf, n=2 * L, axis=1) * spec[None], n=2 * L, axis=1)[:, :L]
    return (y + vf * skip.astype(jnp.float32)).astype(v.dtype)


def hyena_mixer(h, w_in, conv_w, conv_b, f_w1, f_b1, f_fr1, f_w2, f_b2, f_fr2, f_w3, f_b3, skip, w_out):
    L = h.shape[1]
    u = h @ w_in
    up = jnp.pad(u, ((0, 0), (1, 1), (0, 0)))
    u = up[:, :-2] * conv_w[0] + up[:, 1:-1] * conv_w[1] + up[:, 2:] * conv_w[2] + conv_b
    v, g1, g2 = jnp.split(u, 3, axis=-1)
    spec = hyena_filter_spectrum(L, f_w1, f_b1, f_fr1, f_w2, f_b2, f_fr2, f_w3, f_b3)
    z = g1 * long_conv(v, spec[:, 0], skip[0])
    y = g2 * long_conv(z, spec[:, 1], skip[1])
    return y @ w_out


def hier_moe(h, w_rg, b_rg, w_re, b_re, w_gate, w_up, w_down):
    f32 = jnp.float32
    nb, L, d = h.shape
    T = nb * L
    xt = h.reshape(T, d)
    lg = (xt @ w_rg).astype(f32) + b_rg.astype(f32)
    pg = jax.nn.softmax(lg, axis=-1)
    gsel = jnp.argmax(lg, axis=-1).astype(jnp.int32)
    pg_sel = jnp.take_along_axis(pg, gsel[:, None], axis=1)
    le = ((xt @ w_re).astype(f32) + b_re.astype(f32)).reshape(T, N_GROUPS, EXPERTS_PER_GROUP)
    le = jnp.take_along_axis(le, gsel[:, None, None], axis=1)[:, 0]
    pe = jax.nn.softmax(le, axis=-1)
    top_p, top_i = lax.top_k(pe, TOP_K)
    gate = pg_sel * top_p / jnp.sum(top_p, axis=-1, keepdims=True)
    eid = (gsel[:, None] * EXPERTS_PER_GROUP + top_i.astype(jnp.int32)).reshape(-1)
    gw = gate.reshape(-1)
    tok = jnp.repeat(jnp.arange(T, dtype=jnp.int32), TOP_K)
    order = jnp.argsort(eid)
    se = eid[order]
    counts = jnp.bincount(eid, length=N_EXPERTS).astype(jnp.int32)
    start = jnp.cumsum(counts) - counts
    pcounts = (counts + ROW_BLOCK - 1) // ROW_BLOCK * ROW_BLOCK
    pend = jnp.cumsum(pcounts)
    pstart = pend - pcounts
    n_rows = T * TOP_K + N_EXPERTS * ROW_BLOCK
    n_blk = n_rows // ROW_BLOCK
    dest = pstart[se] + jnp.arange(T * TOP_K, dtype=jnp.int32) - start[se]
    row_tok = jnp.full((n_rows,), T, jnp.int32).at[dest].set(tok[order])
    row_w = jnp.zeros((n_rows,), f32).at[dest].set(gw[order])
    blk_e = jnp.minimum(jnp.searchsorted(pend, jnp.arange(n_blk, dtype=jnp.int32) * ROW_BLOCK, side='right'), N_EXPERTS - 1)
    xpad = jnp.concatenate([xt, jnp.zeros((1, d), xt.dtype)], axis=0)
    xb = xpad[row_tok].reshape(n_blk, ROW_BLOCK, d)

    def expert_rows(args):
        xr, e = args
        a = jax.nn.silu(xr @ w_gate[e]) * (xr @ w_up[e])
        return a @ w_down[e]

    yb = lax.map(expert_rows, (xb, blk_e))
    yrows = yb.reshape(n_rows, d) * row_w[:, None].astype(yb.dtype)
    out = jnp.zeros((T + 1, d), yb.dtype).at[row_tok].add(yrows)[:T]
    return out.reshape(nb, L, d).astype(h.dtype)


def encoder_trunk(x, norm_mix, norm_ffn, norm_final, mla_p, hy_p, moe_p):
    for i in range(DEPTH):
        h = rms_norm(x, norm_mix[i])
        j = i // 2
        if i % 2 == 0:
            x = x + mla_mixer(h, *[p[j] for p in mla_p])
        else:
            x = x + hyena_mixer(h, *[p[j] for p in hy_p])
        x = x + hier_moe(rms_norm(x, norm_ffn[i]), *[p[i] for p in moe_p])
    return rms_norm(x, norm_final)


def setup_inputs(seed: int = 0) -> dict:
    key = jax.random.key(seed)
    ks = iter(jax.random.split(key, 40))
    f32 = jnp.float32

    def nrm(shape, scale):
        return jax.random.normal(next(ks), shape, f32) * scale

    D = D_MODEL
    return {
        'x_prompt': nrm((BATCH, SEQ, D), 1.0),
        'x_sample': nrm((DEC_BATCH, DEC_SEQ, D), 1.0),
        'norm_mix': 1.0 + nrm((DEPTH, D), 0.02),
        'norm_ffn': 1.0 + nrm((DEPTH, D), 0.02),
        'norm_final': 1.0 + nrm((D,), 0.02),
        'mla_w_in': nrm((N_MLA, D, Q_RANK + KV_RANK + QK_ROPE), D ** -0.5),
        'mla_q_norm': 1.0 + nrm((N_MLA, Q_RANK), 0.02),
        'mla_w_q_up': nrm((N_MLA, Q_RANK, N_HEADS * (QK_NOPE + QK_ROPE)), Q_RANK ** -0.5),
        'mla_kv_norm': 1.0 + nrm((N_MLA, KV_RANK), 0.02),
        'mla_w_kv_up': nrm((N_MLA, KV_RANK, N_HEADS * (QK_NOPE + V_DIM)), KV_RANK ** -0.5),
        'mla_w_out': nrm((N_MLA, N_HEADS * V_DIM, D), (N_HEADS * V_DIM) ** -0.5),
        'hy_w_in': nrm((N_HYENA, D, 3 * D), D ** -0.5),
        'hy_conv_w': nrm((N_HYENA, 3, 3 * D), 3 ** -0.5),
        'hy_conv_b': nrm((N_HYENA, 3 * D), 0.02),
        'hy_f_w1': nrm((N_HYENA, HY_EMB, HY_HIDDEN), HY_EMB ** -0.5),
        'hy_f_b1': nrm((N_HYENA, HY_HIDDEN), 0.1),
        'hy_f_freq1': 1.0 + nrm((N_HYENA, HY_HIDDEN), 0.02),
        'hy_f_w2': nrm((N_HYENA, HY_HIDDEN, HY_HIDDEN), HY_HIDDEN ** -0.5),
        'hy_f_b2': nrm((N_HYENA, HY_HIDDEN), 0.1),
        'hy_f_freq2': 1.0 + nrm((N_HYENA, HY_HIDDEN), 0.02),
        'hy_f_w3': nrm((N_HYENA, HY_HIDDEN, HY_ORDER * 2 * D), HY_HIDDEN ** -0.5),
        'hy_f_b3': nrm((N_HYENA, HY_ORDER * 2 * D), 0.02),
        'hy_skip': nrm((N_HYENA, HY_ORDER, D), 0.1),
        'hy_w_out': nrm((N_HYENA, D, D), D ** -0.5),
        'moe_w_group': nrm((DEPTH, D, N_GROUPS), D ** -0.5),
        'moe_b_group': nrm((DEPTH, N_GROUPS), 0.01),
        'moe_w_expert': nrm((DEPTH, D, N_EXPERTS), D ** -0.5),
        'moe_b_expert': nrm((DEPTH, N_EXPERTS), 0.01),
        'moe_w_gate': nrm((DEPTH, N_EXPERTS, D, D_EXPERT), D ** -0.5),
        'moe_w_up': nrm((DEPTH, N_EXPERTS, D, D_EXPERT), D ** -0.5),
        'moe_w_down': nrm((DEPTH, N_EXPERTS, D_EXPERT, D), D_EXPERT ** -0.5),
    }


def reference(x_prompt, x_sample, norm_mix, norm_ffn, norm_final,
              mla_w_in, mla_q_norm, mla_w_q_up, mla_kv_norm, mla_w_kv_up, mla_w_out,
              hy_w_in, hy_conv_w, hy_conv_b, hy_f_w1, hy_f_b1, hy_f_freq1, hy_f_w2, hy_f_b2,
              hy_f_freq2, hy_f_w3, hy_f_b3, hy_skip, hy_w_out,
              moe_w_group, moe_b_group, moe_w_expert, moe_b_expert, moe_w_gate, moe_w_up, moe_w_down):
    mla_p = (mla_w_in, mla_q_norm, mla_w_q_up, mla_kv_norm, mla_w_kv_up, mla_w_out)
    hy_p = (hy_w_in, hy_conv_w, hy_conv_b, hy_f_w1, hy_f_b1, hy_f_freq1, hy_f_w2, hy_f_b2,
            hy_f_freq2, hy_f_w3, hy_f_b3, hy_skip, hy_w_out)
    moe_p = (moe_w_group, moe_b_group, moe_w_expert, moe_b_expert, moe_w_gate, moe_w_up, moe_w_down)
    y_prompt = encoder_trunk(x_prompt, norm_mix, norm_ffn, norm_final, mla_p, hy_p, moe_p)
    y_sample = encoder_trunk(x_sample, norm_mix, norm_ffn, norm_final, mla_p, hy_p, moe_p)
    return (y_prompt, y_sample)
```

```python
import functools
import math
import jax
import jax.numpy as jnp
from jax import lax
from jax.experimental import pallas as pl
from jax.experimental.pallas import tpu as pltpu

D_MODEL = 2048
DEPTH = 2
N_HEADS = 16
Q_RANK = 512
KV_RANK = 512
QK_NOPE = 128
QK_ROPE = 64
V_DIM = 128
ROPE_THETA = 10000.0
Q_BLOCK = 128
ATTN_SCALE = 1.0 / math.sqrt(QK_NOPE + QK_ROPE)
HY_ORDER = 2
HY_EMB = 33
HY_BANDS = (HY_EMB - 1) // 2
HY_HIDDEN = 64
HY_FAST_DECAY = 0.3
HY_SLOW_DECAY = 1.5
HY_TARGET = 1e-2
N_GROUPS = 8
EXPERTS_PER_GROUP = 8
N_EXPERTS = N_GROUPS * EXPERTS_PER_GROUP
TOP_K = 2
D_EXPERT = 512
ROW_BLOCK = 128
NORM_EPS = 1e-6


def _rms_kernel(x_ref, g_ref, o_ref):
    xf = x_ref[...]
    y = xf * lax.rsqrt(jnp.mean(xf * xf, axis=-1, keepdims=True) + NORM_EPS)
    o_ref[...] = y * g_ref[...]


def rms_norm_pallas(x, g, rows=512):
    shp = x.shape
    d = shp[-1]
    x2 = x.reshape(-1, d)
    t = x2.shape[0]
    out = pl.pallas_call(
        _rms_kernel,
        grid=(t // rows,),
        in_specs=[pl.BlockSpec((rows, d), lambda i: (i, 0)),
                  pl.BlockSpec((1, d), lambda i: (0, 0))],
        out_specs=pl.BlockSpec((rows, d), lambda i: (i, 0)),
        out_shape=jax.ShapeDtypeStruct((t, d), x.dtype),
    )(x2, g.reshape(1, d))
    return out.reshape(shp)


def rms_norm(x, g):
    xf = x.astype(jnp.float32)
    y = xf * lax.rsqrt(jnp.mean(xf * xf, axis=-1, keepdims=True) + NORM_EPS)
    return (y * g.astype(jnp.float32)).astype(x.dtype)


def rope_tables(L):
    inv = 1.0 / (ROPE_THETA ** (jnp.arange(0, QK_ROPE, 2, dtype=jnp.float32) / QK_ROPE))
    ang = jnp.arange(L, dtype=jnp.float32)[:, None] * inv[None, :]
    ang = jnp.concatenate([ang, ang], axis=-1)
    return jnp.cos(ang), jnp.sin(ang)


def apply_rope(x, cos, sin):
    xf = x.astype(jnp.float32)
    half = QK_ROPE // 2
    rot = jnp.concatenate([-xf[..., half:], xf[..., :half]], axis=-1)
    return (xf * cos + rot * sin).astype(x.dtype)


def blocked_attention(qn, qr, kn, kr, v):
    b, L, h, _ = qn.shape
    nq = L // Q_BLOCK
    qn_b = qn.reshape(b, nq, Q_BLOCK, h, QK_NOPE).transpose(1, 0, 2, 3, 4)
    qr_b = qr.reshape(b, nq, Q_BLOCK, h, QK_ROPE).transpose(1, 0, 2, 3, 4)

    def one_block(args):
        qnb, qrb = args
        s = (jnp.einsum('bqhd,bkhd->bhqk', qnb, kn).astype(jnp.float32)
             + jnp.einsum('bqhr,bkr->bhqk', qrb, kr).astype(jnp.float32))
        p = jax.nn.softmax(s * ATTN_SCALE, axis=-1)
        return jnp.einsum('bhqk,bkhd->bqhd', p.astype(v.dtype), v)

    o = lax.map(one_block, (qn_b, qr_b))
    return o.transpose(1, 0, 2, 3, 4).reshape(b, L, h, V_DIM)


def mla_mixer(h, w_in, q_norm, w_q_up, kv_norm, w_kv_up, w_out):
    b, L, _ = h.shape
    c = h @ w_in
    cq = c[..., :Q_RANK]
    ckv = c[..., Q_RANK:Q_RANK + KV_RANK]
    kr = c[..., Q_RANK + KV_RANK:]
    q = (rms_norm(cq, q_norm) @ w_q_up).reshape(b, L, N_HEADS, QK_NOPE + QK_ROPE)
    kv = (rms_norm(ckv, kv_norm) @ w_kv_up).reshape(b, L, N_HEADS, QK_NOPE + V_DIM)
    cos, sin = rope_tables(L)
    qn = q[..., :QK_NOPE]
    qr = apply_rope(q[..., QK_NOPE:], cos[:, None, :], sin[:, None, :])
    kn = kv[..., :QK_NOPE]
    v = kv[..., QK_NOPE:]
    kr = apply_rope(kr, cos, sin)
    o = blocked_attention(qn, qr, kn, kr, v)
    return o.reshape(b, L, N_HEADS * V_DIM) @ w_out


def hyena_filter_spectrum(L, f_w1, f_b1, f_fr1, f_w2, f_b2, f_fr2, f_w3, f_b3):
    f32 = jnp.float32
    t = jnp.linspace(0.0, 1.0, L, dtype=f32)[:, None]
    w = (2.0 * math.pi / L) * jnp.arange(L, dtype=f32)[:, None]
    bands = jnp.linspace(1e-4, HY_BANDS - 1, HY_BANDS, dtype=f32)[None, :]
    feats = jnp.concatenate([t, jnp.cos(bands * w), -jnp.sin(bands * w)], axis=-1)
    a = jnp.sin(f_fr1.astype(f32) * (feats @ f_w1.astype(f32) + f_b1.astype(f32)))
    a = jnp.sin(f_fr2.astype(f32) * (a @ f_w2.astype(f32) + f_b2.astype(f32)))
    filt = (a @ f_w3.astype(f32) + f_b3.astype(f32)).reshape(L, HY_ORDER, 2, D_MODEL)
    max_decay = math.log(HY_TARGET) / HY_FAST_DECAY
    min_decay = math.log(HY_TARGET) / HY_SLOW_DECAY
    deltas = jnp.abs(jnp.linspace(min_decay, max_decay, D_MODEL, dtype=f32))
    filt = filt * jnp.exp(-t * deltas[None, :])[:, None, None, :]
    filt = filt / (jnp.sum(jnp.abs(filt), axis=0, keepdims=True) + 1e-6)
    fwd = filt[:, :, 0]
    bwd = filt[:, :, 1]
    taps = jnp.concatenate([fwd, jnp.zeros((1, HY_ORDER, D_MODEL), f32), bwd[:0:-1]], axis=0)
    return jnp.fft.rfft(taps, n=2 * L, axis=0)


def long_conv(v, spec, skip):
    L = v.shape[1]
    vf = v.astype(jnp.float32)
    y = jnp.fft.irfft(jnp.fft.rfft(vf, n=2 * L, axis=1) * spec[None], n=2 * L, axis=1)[:, :L]
    return (y + vf * skip.astype(jnp.float32)).astype(v.dtype)


def hyena_mixer(h, w_in, conv_w, conv_b, f_w1, f_b1, f_fr1, f_w2, f_b2, f_fr2, f_w3, f_b3, skip, w_out):
    L = h.shape[1]
    u = h @ w_in
    up = jnp.pad(u, ((0, 0), (1, 1), (0, 0)))
    u = up[:, :-2] * conv_w[0] + up[:, 1:-1] * conv_w[1] + up[:, 2:] * conv_w[2] + conv_b
    v, g1, g2 = jnp.split(u, 3, axis=-1)
    spec = hyena_filter_spectrum(L, f_w1, f_b1, f_fr1, f_w2, f_b2, f_fr2, f_w3, f_b3)
    z = g1 * long_conv(v, spec[:, 0], skip[0])
    y = g2 * long_conv(z, spec[:, 1], skip[1])
    return y @ w_out


def hier_moe(h, w_rg, b_rg, w_re, b_re, w_gate, w_up, w_down):
    f32 = jnp.float32
    nb, L, d = h.shape
    T = nb * L
    xt = h.reshape(T, d)
    lg = (xt @ w_rg).astype(f32) + b_rg.astype(f32)
    pg = jax.nn.softmax(lg, axis=-1)
    gsel = jnp.argmax(lg, axis=-1).astype(jnp.int32)
    pg_sel = jnp.take_along_axis(pg, gsel[:, None], axis=1)
    le = ((xt @ w_re).astype(f32) + b_re.astype(f32)).reshape(T, N_GROUPS, EXPERTS_PER_GROUP)
    le = jnp.take_along_axis(le, gsel[:, None, None], axis=1)[:, 0]
    pe = jax.nn.softmax(le, axis=-1)
    top_p, top_i = lax.top_k(pe, TOP_K)
    gate = pg_sel * top_p / jnp.sum(top_p, axis=-1, keepdims=True)
    eid = (gsel[:, None] * EXPERTS_PER_GROUP + top_i.astype(jnp.int32)).reshape(-1)
    gw = gate.reshape(-1)
    tok = jnp.repeat(jnp.arange(T, dtype=jnp.int32), TOP_K)
    order = jnp.argsort(eid)
    se = eid[order]
    counts = jnp.bincount(eid, length=N_EXPERTS).astype(jnp.int32)
    start = jnp.cumsum(counts) - counts
    pcounts = (counts + ROW_BLOCK - 1) // ROW_BLOCK * ROW_BLOCK
    pend = jnp.cumsum(pcounts)
    pstart = pend - pcounts
    n_rows = T * TOP_K + N_EXPERTS * ROW_BLOCK
    n_blk = n_rows // ROW_BLOCK
    dest = pstart[se] + jnp.arange(T * TOP_K, dtype=jnp.int32) - start[se]
    row_tok = jnp.full((n_rows,), T, jnp.int32).at[dest].set(tok[order])
    row_w = jnp.zeros((n_rows,), f32).at[dest].set(gw[order])
    blk_e = jnp.minimum(jnp.searchsorted(pend, jnp.arange(n_blk, dtype=jnp.int32) * ROW_BLOCK, side='right'), N_EXPERTS - 1)
    xpad = jnp.concatenate([xt, jnp.zeros((1, d), xt.dtype)], axis=0)
    xb = xpad[row_tok].reshape(n_blk, ROW_BLOCK, d)

    def expert_rows(args):
        xr, e = args
        a = jax.nn.silu(xr @ w_gate[e]) * (xr @ w_up[e])
        return a @ w_down[e]

    yb = lax.map(expert_rows, (xb, blk_e))
    yrows = yb.reshape(n_rows, d) * row_w[:, None].astype(yb.dtype)
    out = jnp.zeros((T + 1, d), yb.dtype).at[row_tok].add(yrows)[:T]
    return out.reshape(nb, L, d).astype(h.dtype)


def encoder_trunk(x, norm_mix, norm_ffn, norm_final, mla_p, hy_p, moe_p):
    for i in range(DEPTH):
        h = rms_norm(x, norm_mix[i])
        j = i // 2
        if i % 2 == 0:
            x = x + mla_mixer(h, *[p[j] for p in mla_p])
        else:
            x = x + hyena_mixer(h, *[p[j] for p in hy_p])
        x = x + hier_moe(rms_norm(x, norm_ffn[i]), *[p[i] for p in moe_p])
    return rms_norm_pallas(x, norm_final)


def kernel(x_prompt, x_sample, norm_mix, norm_ffn, norm_final, mla_w_in, mla_q_norm, mla_w_q_up, mla_kv_norm, mla_w_kv_up, mla_w_out, hy_w_in, hy_conv_w, hy_conv_b, hy_f_w1, hy_f_b1, hy_f_freq1, hy_f_w2, hy_f_b2, hy_f_freq2, hy_f_w3, hy_f_b3, hy_skip, hy_w_out, moe_w_group, moe_b_group, moe_w_expert, moe_b_expert, moe_w_gate, moe_w_up, moe_w_down):
    mla_p = (mla_w_in, mla_q_norm, mla_w_q_up, mla_kv_norm, mla_w_kv_up, mla_w_out)
    hy_p = (hy_w_in, hy_conv_w, hy_conv_b, hy_f_w1, hy_f_b1, hy_f_freq1, hy_f_w2, hy_f_b2,
            hy_f_freq2, hy_f_w3, hy_f_b3, hy_skip, hy_w_out)
    moe_p = (moe_w_group, moe_b_group, moe_w_expert, moe_b_expert, moe_w_gate, moe_w_up, moe_w_down)
    y_prompt = encoder_trunk(x_prompt, norm_mix, norm_ffn, norm_final, mla_p, hy_p, moe_p)
    y_sample = encoder_trunk(x_sample, norm_mix, norm_ffn, norm_final, mla_p, hy_p, moe_p)
    return (y_prompt, y_sample)
```

```python
import functools
import math
import jax
import jax.numpy as jnp
from jax import lax
from jax.experimental import pallas as pl
from jax.experimental.pallas import tpu as pltpu

D_MODEL = 2048
DEPTH = 2
N_HEADS = 16
Q_RANK = 512
KV_RANK = 512
QK_NOPE = 128
QK_ROPE = 64
V_DIM = 128
ROPE_THETA = 10000.0
ATTN_SCALE = 1.0 / math.sqrt(QK_NOPE + QK_ROPE)
HY_ORDER = 2
HY_EMB = 33
HY_BANDS = (HY_EMB - 1) // 2
HY_HIDDEN = 64
HY_FAST_DECAY = 0.3
HY_SLOW_DECAY = 1.5
HY_TARGET = 1e-2
N_GROUPS = 8
EXPERTS_PER_GROUP = 8
N_EXPERTS = N_GROUPS * EXPERTS_PER_GROUP
TOP_K = 2
D_EXPERT = 512
ROW_BLOCK = 128
NORM_EPS = 1e-6

F32 = jnp.float32
BF16 = jnp.bfloat16
LANES = 128
HEAD_W = 2 * LANES
VMEM_LIMIT = 56 * 1024 * 1024


def _row_tile(n, want):
    t = min(n, want)
    while n % t:
        t //= 2
    return t


def _rms(x, g):
    return x * lax.rsqrt(jnp.mean(x * x, axis=-1, keepdims=True) + NORM_EPS) * g


def _norm_mm_kernel(x_ref, g_ref, w_ref, o_ref, h_scr):
    @pl.when(pl.program_id(1) == 0)
    def _():
        h_scr[...] = _rms(x_ref[...].astype(F32), g_ref[...]).astype(BF16)

    o_ref[...] = jnp.dot(h_scr[...], w_ref[...], preferred_element_type=F32).astype(o_ref.dtype)


def norm_matmul(x, g, w, out_dtype=F32, tm=512, tn=1024):
    m, k = x.shape
    n = w.shape[1]
    tm = _row_tile(m, tm)
    tn = n if n <= 2048 else _row_tile(n, tn)
    return pl.pallas_call(
        _norm_mm_kernel,
        grid=(m // tm, n // tn),
        in_specs=[pl.BlockSpec((tm, k), lambda i, j: (i, 0)),
                  pl.BlockSpec((1, k), lambda i, j: (0, 0)),
                  pl.BlockSpec((k, tn), lambda i, j: (0, j))],
        out_specs=pl.BlockSpec((tm, tn), lambda i, j: (i, j)),
        out_shape=jax.ShapeDtypeStruct((m, n), out_dtype),
        scratch_shapes=[pltpu.VMEM((tm, k), BF16)],
        compiler_params=pltpu.CompilerParams(
            dimension_semantics=("parallel", "arbitrary"), vmem_limit_bytes=VMEM_LIMIT),
        name="norm_matmul",
    )(x, g.reshape(1, k).astype(F32), w)


def _mm_res_kernel(a_ref, w_ref, r_ref, o_ref):
    acc = jnp.dot(a_ref[...].astype(BF16), w_ref[...], preferred_element_type=F32)
    o_ref[...] = (r_ref[...].astype(F32) + acc).astype(o_ref.dtype)


def matmul_residual(a, w, r, tm=512, tn=1024):
    m, k = a.shape
    n = w.shape[1]
    tm = _row_tile(m, tm)
    tn = _row_tile(n, tn)
    return pl.pallas_call(
        _mm_res_kernel,
        grid=(m // tm, n // tn),
        in_specs=[pl.BlockSpec((tm, k), lambda i, j: (i, 0)),
                  pl.BlockSpec((k, tn), lambda i, j: (0, j)),
                  pl.BlockSpec((tm, tn), lambda i, j: (i, j))],
        out_specs=pl.BlockSpec((tm, tn), lambda i, j: (i, j)),
        out_shape=jax.ShapeDtypeStruct((m, n), r.dtype),
        compiler_params=pltpu.CompilerParams(
            dimension_semantics=("parallel", "parallel"), vmem_limit_bytes=VMEM_LIMIT),
        name="matmul_residual",
    )(a, w, r)


def _rms_kernel(x_ref, g_ref, o_ref):
    o_ref[...] = _rms(x_ref[...].astype(F32), g_ref[...]).astype(o_ref.dtype)


def rms_norm_pallas(x, g, rows=512):
    shp = x.shape
    d = shp[-1]
    x2 = x.reshape(-1, d)
    t = x2.shape[0]
    rows = _row_tile(t, rows)
    out = pl.pallas_call(
        _rms_kernel,
        grid=(t // rows,),
        in_specs=[pl.BlockSpec((rows, d), lambda i: (i, 0)),
                  pl.BlockSpec((1, d), lambda i: (0, 0))],
        out_specs=pl.BlockSpec((rows, d), lambda i: (i, 0)),
        out_shape=jax.ShapeDtypeStruct((t, d), x.dtype),
        compiler_params=pltpu.CompilerParams(dimension_semantics=("parallel",)),
        name="rms_norm",
    )(x2, g.reshape(1, d).astype(F32))
    return out.reshape(shp)


def _mla_qkv_kernel(c_ref, qg_ref, kvg_ref, wq_ref, wkv_ref, cq_ref, sq_ref, ck_ref, sk_ref,
                    q_ref, k_ref, v_ref, *, n_heads, q_rank, kv_rank):
    c = c_ref[...]
    cqn = _rms(c[:, :q_rank], qg_ref[...]).astype(BF16)
    ckvn = _rms(c[:, q_rank:q_rank + kv_rank], kvg_ref[...]).astype(BF16)
    kr = c[:, q_rank + kv_rank:]
    k_hi = (kr * ck_ref[...] + pltpu.roll(kr, LANES // 2, axis=1) * sk_ref[...]).astype(BF16)
    tr = c.shape[0]
    lane = lax.broadcasted_iota(jnp.int32, (tr, LANES), 1)
    v_hi = jnp.where(lane == 0, 1.0, 0.0).astype(BF16)
    cq = cq_ref[...]
    sq = sq_ref[...]
    qw = 3 * LANES
    for h in range(n_heads):
        a = jnp.dot(cqn, wq_ref[:, h * qw:(h + 1) * qw], preferred_element_type=F32)
        q_ref[h, :, :LANES] = (a[:, :LANES] * ATTN_SCALE).astype(BF16)
        q_ref[h, :, LANES:] = (a[:, LANES:2 * LANES] * cq + a[:, 2 * LANES:] * sq).astype(BF16)
        kv = jnp.dot(ckvn, wkv_ref[:, h * HEAD_W:(h + 1) * HEAD_W], preferred_element_type=F32)
        k_ref[h, :, :LANES] = kv[:, :LANES].astype(BF16)
        k_ref[h, :, LANES:] = k_hi
        v_ref[h, :, :LANES] = kv[:, LANES:].astype(BF16)
        v_ref[h, :, LANES:] = v_hi


def _attn_kernel(q_ref, k_ref, v_ref, o_ref, *, tk, unroll):
    q = q_ref[...]
    tq = q.shape[0]
    nk = k_ref.shape[0] // tk

    def body(j, carry):
        m, acc = carry
        off = pl.multiple_of(j * tk, tk)
        s = lax.dot_general(q, k_ref[pl.ds(off, tk), :], (((1,), (1,)), ((), ())),
                            preferred_element_type=F32)
        m_new = jnp.maximum(m, jnp.max(s, axis=1, keepdims=True))
        p = jnp.exp((s - m_new).astype(BF16))
        acc = acc * jnp.exp(m - m_new) + jnp.dot(p, v_ref[pl.ds(off, tk), :],
                                                 preferred_element_type=F32)
        return m_new, acc

    m0 = jnp.full((tq, 1), -1e30, F32)
    acc0 = jnp.zeros((tq, HEAD_W), F32)
    _, acc = lax.fori_loop(0, nk, body, (m0, acc0), unroll=min(unroll, nk))
    o_ref[...] = (acc[:, :LANES] / acc[:, LANES:LANES + 1]).astype(o_ref.dtype)


def mla_attention(q, k, v, tq=512, tk=1024, unroll=16):
    b, nh, L, _ = q.shape
    tq = _row_tile(L, tq)
    tk = _row_tile(L, tk)
    kv_spec = pl.BlockSpec((None, None, L, HEAD_W), lambda bi, h, i: (bi, h, 0, 0))
    return pl.pallas_call(
        functools.partial(_attn_kernel, tk=tk, unroll=unroll),
        grid=(b, nh, L // tq),
        in_specs=[pl.BlockSpec((None, None, tq, HEAD_W), lambda bi, h, i: (bi, h, i, 0)),
                  kv_spec, kv_spec],
        out_specs=pl.BlockSpec((None, tq, V_DIM), lambda bi, h, i: (bi, i, h)),
        out_shape=jax.ShapeDtypeStruct((b, L, nh * V_DIM), BF16),
        compiler_params=pltpu.CompilerParams(
            dimension_semantics=("parallel", "parallel", "arbitrary"),
            vmem_limit_bytes=VMEM_LIMIT),
        name="mla_attention",
    )(q, k, v)


def _rot_cols(w):
    half = w.shape[-1] // 2
    return jnp.concatenate([-w[..., half:], w[..., :half]], axis=-1)


def mla_block(x, norm_g, w_in, q_norm, w_q_up, kv_norm, w_kv_up, w_out, tq=512, tk=1024):
    assert QK_NOPE == LANES and V_DIM == LANES and QK_ROPE == LANES // 2
    b, L, d = x.shape
    t = b * L
    nh = N_HEADS
    x2 = x.reshape(t, d)
    rk = Q_RANK + KV_RANK
    w_in_ext = jnp.concatenate([w_in, _rot_cols(w_in[:, rk:])], axis=1).astype(BF16)
    c = norm_matmul(x2, norm_g, w_in_ext)

    wq = w_q_up.reshape(Q_RANK, nh, QK_NOPE + QK_ROPE)
    rope_w = wq[..., QK_NOPE:]
    z = jnp.zeros_like(rope_w)
    wq3 = jnp.concatenate([wq[..., :QK_NOPE], rope_w, z, _rot_cols(rope_w), z], axis=-1)
    wq3 = wq3.reshape(Q_RANK, nh * 3 * LANES).astype(BF16)
    wkv = w_kv_up.astype(BF16)

    inv = 1.0 / (ROPE_THETA ** (jnp.arange(0, QK_ROPE, 2, dtype=F32) / QK_ROPE))
    ang = jnp.arange(L, dtype=F32)[:, None] * inv[None, :]
    ang = jnp.concatenate([ang, ang], axis=-1)
    zpad = jnp.zeros((L, LANES - QK_ROPE), F32)
    cos_k = jnp.concatenate([jnp.cos(ang), zpad], axis=1)
    sin_k = jnp.concatenate([jnp.sin(ang), zpad], axis=1)
    cos_q = cos_k * ATTN_SCALE
    sin_q = sin_k * ATTN_SCALE

    tr = _row_tile(L, 256)
    nl = L // tr
    cw = c.shape[1]
    tab = pl.BlockSpec((tr, LANES), lambda i: (i % nl, 0))
    slab = pl.BlockSpec((None, nh, tr, HEAD_W), lambda i: (i // nl, 0, i % nl, 0))
    slab_shape = jax.ShapeDtypeStruct((b, nh, L, HEAD_W), BF16)
    q, k, v = pl.pallas_call(
        functools.partial(_mla_qkv_kernel, n_heads=nh, q_rank=Q_RANK, kv_rank=KV_RANK),
        grid=(t // tr,),
        in_specs=[pl.BlockSpec((tr, cw), lambda i: (i, 0)),
                  pl.BlockSpec((1, Q_RANK), lambda i: (0, 0)),
                  pl.BlockSpec((1, KV_RANK), lambda i: (0, 0)),
                  pl.BlockSpec(wq3.shape, lambda i: (0, 0)),
                  pl.BlockSpec(wkv.shape, lambda i: (0, 0)),
                  tab, tab, tab, tab],
        out_specs=[slab, slab, slab],
        out_shape=[slab_shape, slab_shape, slab_shape],
        compiler_params=pltpu.CompilerParams(
            dimension_semantics=("parallel",), vmem_limit_bytes=VMEM_LIMIT),
        name="mla_qkv",
    )(c, q_norm.reshape(1, -1).astype(F32), kv_norm.reshape(1, -1).astype(F32), wq3, wkv,
      cos_q, sin_q, cos_k, sin_k)

    o = mla_attention(q, k, v, tq=tq, tk=tk)

    out = matmul_residual(o.reshape(t, nh * V_DIM), w_out.astype(BF16), x2)
    return out.reshape(b, L, d)


def rms_norm(x, g):
    xf = x.astype(jnp.float32)
    y = xf * lax.rsqrt(jnp.mean(xf * xf, axis=-1, keepdims=True) + NORM_EPS)
    return (y * g.astype(jnp.float32)).astype(x.dtype)


def hyena_filter_spectrum(L, f_w1, f_b1, f_fr1, f_w2, f_b2, f_fr2, f_w3, f_b3):
    f32 = jnp.float32
    t = jnp.linspace(0.0, 1.0, L, dtype=f32)[:, None]
    w = (2.0 * math.pi / L) * jnp.arange(L, dtype=f32)[:, None]
    bands = jnp.linspace(1e-4, HY_BANDS - 1, HY_BANDS, dtype=f32)[None, :]
    feats = jnp.concatenate([t, jnp.cos(bands * w), -jnp.sin(bands * w)], axis=-1)
    a = jnp.sin(f_fr1.astype(f32) * (feats @ f_w1.astype(f32) + f_b1.astype(f32)))
    a = jnp.sin(f_fr2.astype(f32) * (a @ f_w2.astype(f32) + f_b2.astype(f32)))
    filt = (a @ f_w3.astype(f32) + f_b3.astype(f32)).reshape(L, HY_ORDER, 2, D_MODEL)
    max_decay = math.log(HY_TARGET) / HY_FAST_DECAY
    min_decay = math.log(HY_TARGET) / HY_SLOW_DECAY
    deltas = jnp.abs(jnp.linspace(min_decay, max_decay, D_MODEL, dtype=f32))
    filt = filt * jnp.exp(-t * deltas[None, :])[:, None, None, :]
    filt = filt / (jnp.sum(jnp.abs(filt), axis=0, keepdims=True) + 1e-6)
    fwd = filt[:, :, 0]
    bwd = filt[:, :, 1]
    taps = jnp.concatenate([fwd, jnp.zeros((1, HY_ORDER, D_MODEL), f32), bwd[:0:-1]], axis=0)
    return jnp.fft.rfft(taps, n=2 * L, axis=0)


def long_conv(v, spec, skip):
    L = v.shape[1]
    vf = v.astype(jnp.float32)
    y = jnp.fft.irfft(jnp.fft.rfft(vf, n=2 * L, axis=1) * spec[None], n=2 * L, axis=1)[:, :L]
    return (y + vf * skip.astype(jnp.float32)).astype(v.dtype)


def hyena_mixer(h, w_in, conv_w, conv_b, f_w1, f_b1, f_fr1, f_w2, f_b2, f_fr2, f_w3, f_b3, skip, w_out):
    L = h.shape[1]
    u = h @ w_in
    up = jnp.pad(u, ((0, 0), (1, 1), (0, 0)))
    u = up[:, :-2] * conv_w[0] + up[:, 1:-1] * conv_w[1] + up[:, 2:] * conv_w[2] + conv_b
    v, g1, g2 = jnp.split(u, 3, axis=-1)
    spec = hyena_filter_spectrum(L, f_w1, f_b1, f_fr1, f_w2, f_b2, f_fr2, f_w3, f_b3)
    z = g1 * long_conv(v, spec[:, 0], skip[0])
    y = g2 * long_conv(z, spec[:, 1], skip[1])
    return y @ w_out


def hier_moe(h, w_rg, b_rg, w_re, b_re, w_gate, w_up, w_down):
    f32 = jnp.float32
    nb, L, d = h.shape
    T = nb * L
    xt = h.reshape(T, d)
    lg = (xt @ w_rg).astype(f32) + b_rg.astype(f32)
    pg = jax.nn.softmax(lg, axis=-1)
    gsel = jnp.argmax(lg, axis=-1).astype(jnp.int32)
    pg_sel = jnp.take_along_axis(pg, gsel[:, None], axis=1)
    le = ((xt @ w_re).astype(f32) + b_re.astype(f32)).reshape(T, N_GROUPS, EXPERTS_PER_GROUP)
    le = jnp.take_along_axis(le, gsel[:, None, None], axis=1)[:, 0]
    pe = jax.nn.softmax(le, axis=-1)
    top_p, top_i = lax.top_k(pe, TOP_K)
    gate = pg_sel * top_p / jnp.sum(top_p, axis=-1, keepdims=True)
    eid = (gsel[:, None] * EXPERTS_PER_GROUP + top_i.astype(jnp.int32)).reshape(-1)
    gw = gate.reshape(-1)
    tok = jnp.repeat(jnp.arange(T, dtype=jnp.int32), TOP_K)
    order = jnp.argsort(eid)
    se = eid[order]
    counts = jnp.bincount(eid, length=N_EXPERTS).astype(jnp.int32)
    start = jnp.cumsum(counts) - counts
    pcounts = (counts + ROW_BLOCK - 1) // ROW_BLOCK * ROW_BLOCK
    pend = jnp.cumsum(pcounts)
    pstart = pend - pcounts
    n_rows = T * TOP_K + N_EXPERTS * ROW_BLOCK
    n_blk = n_rows // ROW_BLOCK
    dest = pstart[se] + jnp.arange(T * TOP_K, dtype=jnp.int32) - start[se]
    row_tok = jnp.full((n_rows,), T, jnp.int32).at[dest].set(tok[order])
    row_w = jnp.zeros((n_rows,), f32).at[dest].set(gw[order])
    blk_e = jnp.minimum(jnp.searchsorted(pend, jnp.arange(n_blk, dtype=jnp.int32) * ROW_BLOCK, side='right'), N_EXPERTS - 1)
    xpad = jnp.concatenate([xt, jnp.zeros((1, d), xt.dtype)], axis=0)
    xb = xpad[row_tok].reshape(n_blk, ROW_BLOCK, d)

    def expert_rows(args):
        xr, e = args
        a = jax.nn.silu(xr @ w_gate[e]) * (xr @ w_up[e])
        return a @ w_down[e]

    yb = lax.map(expert_rows, (xb, blk_e))
    yrows = yb.reshape(n_rows, d) * row_w[:, None].astype(yb.dtype)
    out = jnp.zeros((T + 1, d), yb.dtype).at[row_tok].add(yrows)[:T]
    return out.reshape(nb, L, d).astype(h.dtype)


def encoder_trunk(x, norm_mix, norm_ffn, norm_final, mla_p, hy_p, moe_p):
    for i in range(DEPTH):
        j = i // 2
        if i % 2 == 0:
            x = mla_block(x, norm_mix[i], *[p[j] for p in mla_p])
        else:
            x = x + hyena_mixer(rms_norm(x, norm_mix[i]), *[p[j] for p in hy_p])
        x = x + hier_moe(rms_norm(x, norm_ffn[i]), *[p[i] for p in moe_p])
    return rms_norm_pallas(x, norm_final)


def kernel(x_prompt, x_sample, norm_mix, norm_ffn, norm_final, mla_w_in, mla_q_norm, mla_w_q_up, mla_kv_norm, mla_w_kv_up, mla_w_out, hy_w_in, hy_conv_w, hy_conv_b, hy_f_w1, hy_f_b1, hy_f_freq1, hy_f_w2, hy_f_b2, hy_f_freq2, hy_f_w3, hy_f_b3, hy_skip, hy_w_out, moe_w_group, moe_b_group, moe_w_expert, moe_b_expert, moe_w_gate, moe_w_up, moe_w_down):
    mla_p = (mla_w_in, mla_q_norm, mla_w_q_up, mla_kv_norm, mla_w_kv_up, mla_w_out)
    hy_p = (hy_w_in, hy_conv_w, hy_conv_b, hy_f_w1, hy_f_b1, hy_f_freq1, hy_f_w2, hy_f_b2,
            hy_f_freq2, hy_f_w3, hy_f_b3, hy_skip, hy_w_out)
    moe_p = (moe_w_group, moe_b_group, moe_w_expert, moe_b_expert, moe_w_gate, moe_w_up, moe_w_down)
    y_prompt = encoder_trunk(x_prompt, norm_mix, norm_ffn, norm_final, mla_p, hy_p, moe_p)
    y_sample = encoder_trunk(x_sample, norm_mix, norm_ffn, norm_final, mla_p, hy_p, moe_p)
    return (y_prompt, y_sample)
```

```python
import functools
import math
import jax
import jax.numpy as jnp
from jax import lax
from jax.experimental import pallas as pl
from jax.experimental.pallas import tpu as pltpu

D_MODEL = 2048
DEPTH = 2
N_HEADS = 16
Q_RANK = 512
KV_RANK = 512
QK_NOPE = 128
QK_ROPE = 64
V_DIM = 128
ROPE_THETA = 10000.0
ATTN_SCALE = 1.0 / math.sqrt(QK_NOPE + QK_ROPE)
HY_ORDER = 2
HY_EMB = 33
HY_BANDS = (HY_EMB - 1) // 2
HY_HIDDEN = 64
HY_FAST_DECAY = 0.3
HY_SLOW_DECAY = 1.5
HY_TARGET = 1e-2
N_GROUPS = 8
EXPERTS_PER_GROUP = 8
N_EXPERTS = N_GROUPS * EXPERTS_PER_GROUP
TOP_K = 2
D_EXPERT = 512
ROW_BLOCK = 128
NORM_EPS = 1e-6

F32 = jnp.float32
BF16 = jnp.bfloat16
LANES = 128
HEAD_W = 2 * LANES
VMEM_LIMIT = 56 * 1024 * 1024


def _row_tile(n, want):
    t = min(n, want)
    while n % t:
        t //= 2
    return t


def _rms(x, g):
    return x * lax.rsqrt(jnp.mean(x * x, axis=-1, keepdims=True) + NORM_EPS) * g


def _norm_mm_kernel(x_ref, g_ref, w_ref, o_ref, h_scr):
    @pl.when(pl.program_id(1) == 0)
    def _():
        h_scr[...] = _rms(x_ref[...].astype(F32), g_ref[...]).astype(BF16)

    o_ref[...] = jnp.dot(h_scr[...], w_ref[...], preferred_element_type=F32).astype(o_ref.dtype)


def norm_matmul(x, g, w, out_dtype=F32, tm=512, tn=1024):
    m, k = x.shape
    n = w.shape[1]
    tm = _row_tile(m, tm)
    tn = n if n <= 2048 else _row_tile(n, tn)
    return pl.pallas_call(
        _norm_mm_kernel,
        grid=(m // tm, n // tn),
        in_specs=[pl.BlockSpec((tm, k), lambda i, j: (i, 0)),
                  pl.BlockSpec((1, k), lambda i, j: (0, 0)),
                  pl.BlockSpec((k, tn), lambda i, j: (0, j))],
        out_specs=pl.BlockSpec((tm, tn), lambda i, j: (i, j)),
        out_shape=jax.ShapeDtypeStruct((m, n), out_dtype),
        scratch_shapes=[pltpu.VMEM((tm, k), BF16)],
        compiler_params=pltpu.CompilerParams(
            dimension_semantics=("parallel", "arbitrary"), vmem_limit_bytes=VMEM_LIMIT),
        name="norm_matmul",
    )(x, g.reshape(1, k).astype(F32), w)


def _mm_res_kernel(a_ref, w_ref, r_ref, o_ref):
    acc = jnp.dot(a_ref[...].astype(BF16), w_ref[...], preferred_element_type=F32)
    o_ref[...] = (r_ref[...].astype(F32) + acc).astype(o_ref.dtype)


def matmul_residual(a, w, r, tm=512, tn=1024):
    m, k = a.shape
    n = w.shape[1]
    tm = _row_tile(m, tm)
    tn = _row_tile(n, tn)
    return pl.pallas_call(
        _mm_res_kernel,
        grid=(m // tm, n // tn),
        in_specs=[pl.BlockSpec((tm, k), lambda i, j: (i, 0)),
                  pl.BlockSpec((k, tn), lambda i, j: (0, j)),
                  pl.BlockSpec((tm, tn), lambda i, j: (i, j))],
        out_specs=pl.BlockSpec((tm, tn), lambda i, j: (i, j)),
        out_shape=jax.ShapeDtypeStruct((m, n), r.dtype),
        compiler_params=pltpu.CompilerParams(
            dimension_semantics=("parallel", "parallel"), vmem_limit_bytes=VMEM_LIMIT),
        name="matmul_residual",
    )(a, w, r)


def _rms_kernel(x_ref, g_ref, o_ref):
    o_ref[...] = _rms(x_ref[...].astype(F32), g_ref[...]).astype(o_ref.dtype)


def rms_norm_pallas(x, g, rows=512):
    shp = x.shape
    d = shp[-1]
    x2 = x.reshape(-1, d)
    t = x2.shape[0]
    rows = _row_tile(t, rows)
    out = pl.pallas_call(
        _rms_kernel,
        grid=(t // rows,),
        in_specs=[pl.BlockSpec((rows, d), lambda i: (i, 0)),
                  pl.BlockSpec((1, d), lambda i: (0, 0))],
        out_specs=pl.BlockSpec((rows, d), lambda i: (i, 0)),
        out_shape=jax.ShapeDtypeStruct((t, d), x.dtype),
        compiler_params=pltpu.CompilerParams(dimension_semantics=("parallel",)),
        name="rms_norm",
    )(x2, g.reshape(1, d).astype(F32))
    return out.reshape(shp)


def _mla_qkv_kernel(c_ref, qg_ref, kvg_ref, wq_ref, wkv_ref, cq_ref, sq_ref, ck_ref, sk_ref,
                    q_ref, k_ref, v_ref, *, n_heads, q_rank, kv_rank):
    c = c_ref[...]
    cqn = _rms(c[:, :q_rank], qg_ref[...]).astype(BF16)
    ckvn = _rms(c[:, q_rank:q_rank + kv_rank], kvg_ref[...]).astype(BF16)
    kr = c[:, q_rank + kv_rank:]
    k_hi = (kr * ck_ref[...] + pltpu.roll(kr, LANES // 2, axis=1) * sk_ref[...]).astype(BF16)
    tr = c.shape[0]
    lane = lax.broadcasted_iota(jnp.int32, (tr, LANES), 1)
    v_hi = jnp.where(lane == 0, 1.0, 0.0).astype(BF16)
    cq = cq_ref[...]
    sq = sq_ref[...]
    qw = 3 * LANES
    for h in range(n_heads):
        a = jnp.dot(cqn, wq_ref[:, h * qw:(h + 1) * qw], preferred_element_type=F32)
        q_ref[h, :, :LANES] = (a[:, :LANES] * ATTN_SCALE).astype(BF16)
        q_ref[h, :, LANES:] = (a[:, LANES:2 * LANES] * cq + a[:, 2 * LANES:] * sq).astype(BF16)
        kv = jnp.dot(ckvn, wkv_ref[:, h * HEAD_W:(h + 1) * HEAD_W], preferred_element_type=F32)
        k_ref[h, :, :LANES] = kv[:, :LANES].astype(BF16)
        k_ref[h, :, LANES:] = k_hi
        v_ref[h, :, :LANES] = kv[:, LANES:].astype(BF16)
        v_ref[h, :, LANES:] = v_hi


def _attn_kernel(q_ref, k_ref, v_ref, o_ref, *, tk, unroll):
    q = q_ref[...]
    tq = q.shape[0]
    nk = k_ref.shape[0] // tk

    def body(j, carry):
        m, acc = carry
        off = pl.multiple_of(j * tk, tk)
        s = lax.dot_general(q, k_ref[pl.ds(off, tk), :], (((1,), (1,)), ((), ())),
                            preferred_element_type=F32)
        m_new = jnp.maximum(m, jnp.max(s, axis=1, keepdims=True))
        p = jnp.exp((s - m_new).astype(BF16))
        acc = acc * jnp.exp(m - m_new) + jnp.dot(p, v_ref[pl.ds(off, tk), :],
                                                 preferred_element_type=F32)
        return m_new, acc

    m0 = jnp.full((tq, 1), -1e30, F32)
    acc0 = jnp.zeros((tq, HEAD_W), F32)
    _, acc = lax.fori_loop(0, nk, body, (m0, acc0), unroll=min(unroll, nk))
    o_ref[...] = (acc[:, :LANES] / acc[:, LANES:LANES + 1]).astype(o_ref.dtype)


def mla_attention(q, k, v, tq=512, tk=1024, unroll=16):
    b, nh, L, _ = q.shape
    tq = _row_tile(L, tq)
    tk = _row_tile(L, tk)
    kv_spec = pl.BlockSpec((None, None, L, HEAD_W), lambda bi, h, i: (bi, h, 0, 0))
    return pl.pallas_call(
        functools.partial(_attn_kernel, tk=tk, unroll=unroll),
        grid=(b, nh, L // tq),
        in_specs=[pl.BlockSpec((None, None, tq, HEAD_W), lambda bi, h, i: (bi, h, i, 0)),
                  kv_spec, kv_spec],
        out_specs=pl.BlockSpec((None, tq, V_DIM), lambda bi, h, i: (bi, i, h)),
        out_shape=jax.ShapeDtypeStruct((b, L, nh * V_DIM), BF16),
        compiler_params=pltpu.CompilerParams(
            dimension_semantics=("parallel", "parallel", "arbitrary"),
            vmem_limit_bytes=VMEM_LIMIT),
        name="mla_attention",
    )(q, k, v)


def _rot_cols(w):
    half = w.shape[-1] // 2
    return jnp.concatenate([-w[..., half:], w[..., :half]], axis=-1)


def mla_block(x, norm_g, w_in, q_norm, w_q_up, kv_norm, w_kv_up, w_out, tq=512, tk=1024):
    assert QK_NOPE == LANES and V_DIM == LANES and QK_ROPE == LANES // 2
    b, L, d = x.shape
    t = b * L
    nh = N_HEADS
    x2 = x.reshape(t, d)
    rk = Q_RANK + KV_RANK
    w_in_ext = jnp.concatenate([w_in, _rot_cols(w_in[:, rk:])], axis=1).astype(BF16)
    c = norm_matmul(x2, norm_g, w_in_ext)

    wq = w_q_up.reshape(Q_RANK, nh, QK_NOPE + QK_ROPE)
    rope_w = wq[..., QK_NOPE:]
    z = jnp.zeros_like(rope_w)
    wq3 = jnp.concatenate([wq[..., :QK_NOPE], rope_w, z, _rot_cols(rope_w), z], axis=-1)
    wq3 = wq3.reshape(Q_RANK, nh * 3 * LANES).astype(BF16)
    wkv = w_kv_up.astype(BF16)

    inv = 1.0 / (ROPE_THETA ** (jnp.arange(0, QK_ROPE, 2, dtype=F32) / QK_ROPE))
    ang = jnp.arange(L, dtype=F32)[:, None] * inv[None, :]
    ang = jnp.concatenate([ang, ang], axis=-1)
    zpad = jnp.zeros((L, LANES - QK_ROPE), F32)
    cos_k = jnp.concatenate([jnp.cos(ang), zpad], axis=1)
    sin_k = jnp.concatenate([jnp.sin(ang), zpad], axis=1)
    cos_q = cos_k * ATTN_SCALE
    sin_q = sin_k * ATTN_SCALE

    tr = _row_tile(L, 256)
    nl = L // tr
    cw = c.shape[1]
    tab = pl.BlockSpec((tr, LANES), lambda i: (i % nl, 0))
    slab = pl.BlockSpec((None, nh, tr, HEAD_W), lambda i: (i // nl, 0, i % nl, 0))
    slab_shape = jax.ShapeDtypeStruct((b, nh, L, HEAD_W), BF16)
    q, k, v = pl.pallas_call(
        functools.partial(_mla_qkv_kernel, n_heads=nh, q_rank=Q_RANK, kv_rank=KV_RANK),
        grid=(t // tr,),
        in_specs=[pl.BlockSpec((tr, cw), lambda i: (i, 0)),
                  pl.BlockSpec((1, Q_RANK), lambda i: (0, 0)),
                  pl.BlockSpec((1, KV_RANK), lambda i: (0, 0)),
                  pl.BlockSpec(wq3.shape, lambda i: (0, 0)),
                  pl.BlockSpec(wkv.shape, lambda i: (0, 0)),
                  tab, tab, tab, tab],
        out_specs=[slab, slab, slab],
        out_shape=[slab_shape, slab_shape, slab_shape],
        compiler_params=pltpu.CompilerParams(
            dimension_semantics=("parallel",), vmem_limit_bytes=VMEM_LIMIT),
        name="mla_qkv",
    )(c, q_norm.reshape(1, -1).astype(F32), kv_norm.reshape(1, -1).astype(F32), wq3, wkv,
      cos_q, sin_q, cos_k, sin_k)

    o = mla_attention(q, k, v, tq=tq, tk=tk)

    out = matmul_residual(o.reshape(t, nh * V_DIM), w_out.astype(BF16), x2)
    return out.reshape(b, L, d)


def _fft_split(L):
    n = 2 * L
    n2 = min(256, n // 16)
    return n // n2, n2


def _fft_tables(L):
    n1, n2 = _fft_split(L)
    n = n1 * n2
    n1h = n1 // 2
    k1 = jnp.arange(n1, dtype=jnp.int32)
    m1 = jnp.arange(n1h, dtype=jnp.int32)
    m2 = jnp.arange(n2, dtype=jnp.int32)
    idx = (m2[:, None, None] * k1[None, :, None] + n2 * k1[None, :, None] * m1[None, None, :]) % n
    th = idx.astype(F32) * (2.0 * math.pi / n)
    c, s = jnp.cos(th), jnp.sin(th)
    fa = jnp.concatenate([c, -s], axis=1).astype(BF16)
    ct, st = jnp.swapaxes(c, 1, 2), jnp.swapaxes(s, 1, 2)
    gc = (jnp.concatenate([ct, -st], axis=2) / n1).astype(BF16)
    idx2 = (m2[:, None] * m2[None, :]) % n2
    th2 = idx2.astype(F32) * (2.0 * math.pi / n2)
    c2, s2 = jnp.cos(th2), jnp.sin(th2)
    mf = jnp.concatenate([jnp.concatenate([c2, s2], 1), jnp.concatenate([-s2, c2], 1)], 0).astype(BF16)
    mi = (jnp.concatenate([jnp.concatenate([c2, -s2], 1), jnp.concatenate([s2, c2], 1)], 0) / n2).astype(BF16)
    return fa, gc, mf, mi


def _fft_a_kernel(x_ref, fa_ref, y_ref):
    n1 = y_ref.shape[1]
    y = jnp.dot(fa_ref[...], x_ref[...].astype(BF16), preferred_element_type=F32)
    y_ref[0] = y[:n1].astype(y_ref.dtype)
    y_ref[1] = y[n1:].astype(y_ref.dtype)


def fft_stage_a(x, fa):
    s, L, d = x.shape
    n2, n1x2, n1h = fa.shape
    n1 = n1x2 // 2
    xv = x.reshape(s, n1h, n2 * d)
    return pl.pallas_call(
        _fft_a_kernel,
        grid=(s, n2),
        in_specs=[pl.BlockSpec((None, n1h, d), lambda b, j: (b, 0, j)),
                  pl.BlockSpec((None, n1x2, n1h), lambda b, j: (j, 0, 0))],
        out_specs=pl.BlockSpec((None, 2, n1, d), lambda b, j: (b, 0, 0, j)),
        out_shape=jax.ShapeDtypeStruct((s, 2, n1, n2 * d), BF16),
        compiler_params=pltpu.CompilerParams(dimension_semantics=("parallel", "parallel")),
        name="fft_stage_a",
    )(xv, fa)


def _fft_b_kernel(y_ref, h_ref, mf_ref, mi_ref, u_ref):
    n2 = y_ref.shape[1]
    tc = y_ref.shape[2]
    z = jnp.dot(mf_ref[...], y_ref[...].reshape(2 * n2, tc), preferred_element_type=F32)
    zr, zi = z[:n2], z[n2:]
    hr, hi = h_ref[0], h_ref[1]
    w = jnp.concatenate([zr * hr - zi * hi, zr * hi + zi * hr], axis=0).astype(BF16)
    u = jnp.dot(mi_ref[...], w, preferred_element_type=F32)
    u_ref[0] = u[:n2].astype(u_ref.dtype)
    u_ref[1] = u[n2:].astype(u_ref.dtype)


def fft_stage_b(y, h, mf, mi, tc=1024):
    s, _, n1, nd = y.shape
    n2 = mf.shape[0] // 2
    d = nd // n2
    tc = _row_tile(d, tc)
    yv = y.reshape(s, 2, n1, n2, d)
    blk = pl.BlockSpec((None, 2, None, n2, tc), lambda k, j, b: (b, 0, k, 0, j))
    u = pl.pallas_call(
        _fft_b_kernel,
        grid=(n1, d // tc, s),
        in_specs=[blk,
                  pl.BlockSpec((2, None, n2, tc), lambda k, j, b: (0, k, 0, j)),
                  pl.BlockSpec(mf.shape, lambda k, j, b: (0, 0)),
                  pl.BlockSpec(mi.shape, lambda k, j, b: (0, 0))],
        out_specs=blk,
        out_shape=jax.ShapeDtypeStruct(yv.shape, BF16),
        compiler_params=pltpu.CompilerParams(
            dimension_semantics=("parallel", "parallel", "arbitrary"), vmem_limit_bytes=VMEM_LIMIT),
        name="fft_stage_b",
    )(yv, h, mf, mi)
    return u.reshape(y.shape)


def _fft_c_kernel(u_ref, gc_ref, v_ref, g_ref, sk_ref, o_ref):
    n1 = u_ref.shape[1]
    d = u_ref.shape[2]
    y = jnp.dot(gc_ref[...], u_ref[...].reshape(2 * n1, d), preferred_element_type=F32)
    v = v_ref[...].astype(F32)
    o_ref[...] = (g_ref[...].astype(F32) * (y + v * sk_ref[...])).astype(o_ref.dtype)


def fft_stage_c(u, gc, v, gate, skip):
    s, _, n1, nd = u.shape
    n2, n1h, _ = gc.shape
    d = nd // n2
    L = n1h * n2
    row = pl.BlockSpec((None, n1h, d), lambda b, j: (b, 0, j))
    out = pl.pallas_call(
        _fft_c_kernel,
        grid=(s, n2),
        in_specs=[pl.BlockSpec((None, 2, n1, d), lambda b, j: (b, 0, 0, j)),
                  pl.BlockSpec((None, n1h, 2 * n1), lambda b, j: (j, 0, 0)),
                  row, row,
                  pl.BlockSpec((1, d), lambda b, j: (0, 0))],
        out_specs=row,
        out_shape=jax.ShapeDtypeStruct((s, n1h, n2 * d), BF16),
        compiler_params=pltpu.CompilerParams(dimension_semantics=("parallel", "parallel")),
        name="fft_stage_c",
    )(u, gc, v.reshape(s, n1h, n2 * d), gate.reshape(s, n1h, n2 * d), skip.reshape(1, d).astype(F32))
    return out.reshape(s, L, d)


def _filter_kernel(ft_ref, w1_ref, b1_ref, f1_ref, w2_ref, b2_ref, f2_ref, w3_ref, b3_ref, dl_ref,
                   taps_ref, sum_ref, *, seq_len):
    i = pl.program_id(1)
    tl = ft_ref.shape[0]
    a = jnp.sin(f1_ref[...] * (jnp.dot(ft_ref[...], w1_ref[...], preferred_element_type=F32) + b1_ref[...]))
    a = jnp.sin(f2_ref[...] * (jnp.dot(a, w2_ref[...], preferred_element_type=F32) + b2_ref[...]))
    f = jnp.dot(a, w3_ref[...], preferred_element_type=F32) + b3_ref[...]
    pos = (lax.broadcasted_iota(jnp.int32, (tl, 1), 0) + i * tl).astype(F32)
    t = pos * (1.0 / max(seq_len - 1, 1))
    f = f * jnp.exp(-t * dl_ref[...])

    @pl.when(i == 0)
    def _():
        sum_ref[...] = jnp.zeros_like(sum_ref)

    sum_ref[...] += jnp.sum(jnp.abs(f), axis=0, keepdims=True)
    is_bwd = pl.program_id(0) % 2 == 1
    f = jnp.where(jnp.logical_and(is_bwd, pos == 0.0), 0.0, f)
    taps_ref[...] = f.astype(taps_ref.dtype)


def hyena_filter_taps(L, d, f_w1, f_b1, f_fr1, f_w2, f_b2, f_fr2, f_w3, f_b3, tl=512):
    hp = LANES
    t = jnp.linspace(0.0, 1.0, L, dtype=F32)[:, None]
    w = (2.0 * math.pi / L) * jnp.arange(L, dtype=F32)[:, None]
    bands = jnp.linspace(1e-4, HY_BANDS - 1, HY_BANDS, dtype=F32)[None, :]
    feats = jnp.concatenate([t, jnp.cos(bands * w), -jnp.sin(bands * w),
                             jnp.zeros((L, hp - HY_EMB), F32)], axis=-1)

    def pad2(m, r, c):
        return jnp.pad(m.astype(F32), ((0, r - m.shape[0]), (0, c - m.shape[1])))

    w1 = pad2(f_w1, hp, hp)
    w2 = pad2(f_w2, hp, hp)
    w3 = pad2(f_w3, hp, f_w3.shape[1])
    b1, fr1 = pad2(f_b1[None], 1, hp), pad2(f_fr1[None], 1, hp)
    b2, fr2 = pad2(f_b2[None], 1, hp), pad2(f_fr2[None], 1, hp)
    max_decay = math.log(HY_TARGET) / HY_FAST_DECAY
    min_decay = math.log(HY_TARGET) / HY_SLOW_DECAY
    deltas = jnp.abs(jnp.linspace(min_decay, max_decay, d, dtype=F32))[None, :]
    ns = HY_ORDER * 2
    tl = _row_tile(L, tl)
    vec = lambda n: pl.BlockSpec((1, n), lambda s, i: (0, 0))
    mat = lambda r, c: pl.BlockSpec((r, c), lambda s, i: (0, 0))
    return pl.pallas_call(
        functools.partial(_filter_kernel, seq_len=L),
        grid=(ns, L // tl),
        in_specs=[pl.BlockSpec((tl, hp), lambda s, i: (i, 0)),
                  mat(hp, hp), vec(hp), vec(hp), mat(hp, hp), vec(hp), vec(hp),
                  pl.BlockSpec((hp, d), lambda s, i: (0, s)),
                  pl.BlockSpec((1, d), lambda s, i: (0, s)),
                  vec(d)],
        out_specs=[pl.BlockSpec((None, tl, d), lambda s, i: (s, i, 0)),
                   pl.BlockSpec((None, 1, d), lambda s, i: (s, 0, 0))],
        out_shape=[jax.ShapeDtypeStruct((ns, L, d), BF16), jax.ShapeDtypeStruct((ns, 1, d), F32)],
        compiler_params=pltpu.CompilerParams(dimension_semantics=("parallel", "arbitrary")),
        name="hyena_filter_taps",
    )(feats, w1, b1, fr1, w2, b2, fr2, w3, f_b3[None].astype(F32), deltas)


def _fft_bf_kernel(yf_ref, yb_ref, sf_ref, sb_ref, mf_ref, h_ref):
    n2 = yf_ref.shape[1]
    tc = yf_ref.shape[2]
    zf = jnp.dot(mf_ref[...], yf_ref[...].reshape(2 * n2, tc), preferred_element_type=F32)
    zb = jnp.dot(mf_ref[...], yb_ref[...].reshape(2 * n2, tc), preferred_element_type=F32)
    sf = 1.0 / (sf_ref[...] + 1e-6)
    sb = 1.0 / (sb_ref[...] + 1e-6)
    h_ref[0] = zf[:n2] * sf + zb[:n2] * sb
    h_ref[1] = zf[n2:] * sf - zb[n2:] * sb


def hyena_spectrum(L, d, fa, mf, *filter_params, tc=1024):
    taps, sums = hyena_filter_taps(L, d, *filter_params)
    y = fft_stage_a(taps, fa)
    ns, _, n1, nd = y.shape
    n2 = nd // d
    tc = _row_tile(d, tc)
    yv = y.reshape(ns, 2, n1, n2, d)
    side = lambda sd: pl.BlockSpec((None, 2, None, n2, tc), lambda o, k, j: (2 * o + sd, 0, k, 0, j))
    norm = lambda sd: pl.BlockSpec((None, 1, tc), lambda o, k, j: (2 * o + sd, 0, j))
    return pl.pallas_call(
        _fft_bf_kernel,
        grid=(HY_ORDER, n1, d // tc),
        in_specs=[side(0), side(1), norm(0), norm(1), pl.BlockSpec(mf.shape, lambda o, k, j: (0, 0))],
        out_specs=pl.BlockSpec((None, 2, None, n2, tc), lambda o, k, j: (o, 0, k, 0, j)),
        out_shape=jax.ShapeDtypeStruct((HY_ORDER, 2, n1, n2, d), F32),
        compiler_params=pltpu.CompilerParams(
            dimension_semantics=("parallel", "parallel", "parallel"), vmem_limit_bytes=VMEM_LIMIT),
        name="fft_stage_b_filter",
    )(yv, yv, sums, sums, mf)


def _shortconv_kernel(u_ref, up_ref, un_ref, w_ref, b_ref, o_ref):
    i = pl.program_id(1)
    u = u_ref[...].astype(F32)
    tr = u.shape[0]
    row = lax.broadcasted_iota(jnp.int32, u.shape, 0)
    prev_row = jnp.where(i == 0, 0.0, up_ref[7:8, :].astype(F32))
    next_row = jnp.where(i == pl.num_programs(1) - 1, 0.0, un_ref[0:1, :].astype(F32))
    um = jnp.where(row == 0, prev_row, pltpu.roll(u, 1, axis=0))
    up = jnp.where(row == tr - 1, next_row, pltpu.roll(u, tr - 1, axis=0))
    w = w_ref[...]
    o_ref[...] = (um * w[0:1] + u * w[1:2] + up * w[2:3] + b_ref[...]).astype(o_ref.dtype)


def hyena_shortconv(u, conv_w, conv_b, tr=256):
    b, L, d3 = u.shape
    d = d3 // 3
    tr = _row_tile(L, tr)
    hb = tr // 8
    nhb = L // 8
    return pl.pallas_call(
        _shortconv_kernel,
        grid=(b, L // tr, 3),
        in_specs=[pl.BlockSpec((None, tr, d), lambda bi, i, j: (bi, i, j)),
                  pl.BlockSpec((None, 8, d), lambda bi, i, j: (bi, jnp.maximum(i * hb - 1, 0), j)),
                  pl.BlockSpec((None, 8, d), lambda bi, i, j: (bi, jnp.minimum((i + 1) * hb, nhb - 1), j)),
                  pl.BlockSpec((3, d), lambda bi, i, j: (0, j)),
                  pl.BlockSpec((1, d), lambda bi, i, j: (0, j))],
        out_specs=pl.BlockSpec((None, None, tr, d), lambda bi, i, j: (j, bi, i, 0)),
        out_shape=jax.ShapeDtypeStruct((3, b, L, d), BF16),
        compiler_params=pltpu.CompilerParams(dimension_semantics=("parallel", "parallel", "parallel")),
        name="hyena_shortconv",
    )(u, u, u, conv_w.astype(F32), conv_b[None].astype(F32))


def hyena_block(x, norm_g, w_in, conv_w, conv_b, f_w1, f_b1, f_fr1, f_w2, f_b2, f_fr2, f_w3, f_b3,
                skip, w_out):
    b, L, d = x.shape
    t = b * L
    x2 = x.reshape(t, d)
    u = norm_matmul(x2, norm_g, w_in.astype(BF16), out_dtype=BF16).reshape(b, L, 3 * d)
    vg = hyena_shortconv(u, conv_w, conv_b)
    v, g1, g2 = vg[0], vg[1], vg[2]
    fa, gc, mf, mi = _fft_tables(L)
    spec = hyena_spectrum(L, d, fa, mf, f_w1, f_b1, f_fr1, f_w2, f_b2, f_fr2, f_w3, f_b3)
    z = fft_stage_c(fft_stage_b(fft_stage_a(v, fa), spec[0], mf, mi), gc, v, g1, skip[0])
    y = fft_stage_c(fft_stage_b(fft_stage_a(z, fa), spec[1], mf, mi), gc, z, g2, skip[1])
    out = matmul_residual(y.reshape(t, d), w_out.astype(BF16), x2)
    return out.reshape(b, L, d)


def _router_kernel(x_ref, g_ref, w_ref, lg_ref, h_ref):
    h = _rms(x_ref[...].astype(F32), g_ref[...]).astype(BF16)
    h_ref[...] = h
    lg_ref[...] = jnp.dot(h, w_ref[...], preferred_element_type=F32)


def _expert_kernel(be_ref, nu_ref, x_ref, wg_ref, wu_ref, wd_ref, rw_ref, o_ref):
    del be_ref
    used = pl.program_id(0) < nu_ref[0]

    @pl.when(used)
    def _():
        x = x_ref[...]
        g = jnp.dot(x, wg_ref[...], preferred_element_type=F32)
        u = jnp.dot(x, wu_ref[...], preferred_element_type=F32)
        a = (g * jax.nn.sigmoid(g) * u).astype(BF16)
        y = jnp.dot(a, wd_ref[...], preferred_element_type=F32)
        o_ref[...] = (y * rw_ref[...]).astype(o_ref.dtype)

    @pl.when(jnp.logical_not(used))
    def _():
        o_ref[...] = jnp.zeros_like(o_ref)


def moe_block(x, norm_g, w_rg, b_rg, w_re, b_re, w_gate, w_up, w_down, tm=256):
    nb, L, d = x.shape
    T = nb * L
    ne = N_EXPERTS
    x2 = x.reshape(T, d)
    nr = N_GROUPS + ne
    w_r = jnp.concatenate([w_rg, w_re, jnp.zeros((d, LANES - nr), w_rg.dtype)], axis=1).astype(BF16)
    tr = _row_tile(T, 512)
    logits, h = pl.pallas_call(
        _router_kernel,
        grid=(T // tr,),
        in_specs=[pl.BlockSpec((tr, d), lambda i: (i, 0)),
                  pl.BlockSpec((1, d), lambda i: (0, 0)),
                  pl.BlockSpec((d, LANES), lambda i: (0, 0))],
        out_specs=[pl.BlockSpec((tr, LANES), lambda i: (i, 0)),
                   pl.BlockSpec((tr, d), lambda i: (i, 0))],
        out_shape=[jax.ShapeDtypeStruct((T, LANES), F32), jax.ShapeDtypeStruct((T, d), BF16)],
        compiler_params=pltpu.CompilerParams(dimension_semantics=("parallel",)),
        name="moe_router",
    )(x2, norm_g.reshape(1, d).astype(F32), w_r)

    lg = logits[:, :N_GROUPS] + b_rg.astype(F32)
    pg = jax.nn.softmax(lg, axis=-1)
    gsel = jnp.argmax(lg, axis=-1).astype(jnp.int32)
    pg_sel = jnp.take_along_axis(pg, gsel[:, None], axis=1)
    le = (logits[:, N_GROUPS:nr] + b_re.astype(F32)).reshape(T, N_GROUPS, EXPERTS_PER_GROUP)
    le = jnp.take_along_axis(le, gsel[:, None, None], axis=1)[:, 0]
    pe = jax.nn.softmax(le, axis=-1)
    top_p, top_i = lax.top_k(pe, TOP_K)
    gate = pg_sel * top_p / jnp.sum(top_p, axis=-1, keepdims=True)
    eid = (gsel[:, None] * EXPERTS_PER_GROUP + top_i.astype(jnp.int32)).reshape(-1)
    gw = gate.reshape(-1)

    na = T * TOP_K
    order = jnp.argsort(eid)
    se = eid[order]
    counts = jnp.bincount(eid, length=ne).astype(jnp.int32)
    start = jnp.cumsum(counts) - counts
    pcounts = (counts + tm - 1) // tm * tm
    pend = jnp.cumsum(pcounts)
    pstart = pend - pcounts
    n_rows = na + ne * tm
    n_blk = n_rows // tm
    dest = pstart[se] + jnp.arange(na, dtype=jnp.int32) - start[se]
    row_tok = jnp.zeros((n_rows,), jnp.int32).at[dest].set(order // TOP_K)
    row_w = jnp.zeros((n_rows,), F32).at[dest].set(gw[order])
    pos = jnp.zeros((na,), jnp.int32).at[order].set(dest)
    blk_e = jnp.minimum(jnp.searchsorted(pend, jnp.arange(n_blk, dtype=jnp.int32) * tm, side='right'),
                        ne - 1).astype(jnp.int32)
    n_used = (pend[-1:] // tm).astype(jnp.int32)

    xs = h[row_tok]
    de = w_gate.shape[-1]
    y = pl.pallas_call(
        _expert_kernel,
        grid_spec=pltpu.PrefetchScalarGridSpec(
            num_scalar_prefetch=2,
            grid=(n_blk,),
            in_specs=[pl.BlockSpec((tm, d), lambda i, be, nu: (i, 0)),
                      pl.BlockSpec((None, d, de), lambda i, be, nu: (be[i], 0, 0)),
                      pl.BlockSpec((None, d, de), lambda i, be, nu: (be[i], 0, 0)),
                      pl.BlockSpec((None, de, d), lambda i, be, nu: (be[i], 0, 0)),
                      pl.BlockSpec((tm, 1), lambda i, be, nu: (i, 0))],
            out_specs=pl.BlockSpec((tm, d), lambda i, be, nu: (i, 0))),
        out_shape=jax.ShapeDtypeStruct((n_rows, d), BF16),
        compiler_params=pltpu.CompilerParams(
            dimension_semantics=("arbitrary",), vmem_limit_bytes=VMEM_LIMIT),
        name="moe_experts",
    )(blk_e, n_used, xs, w_gate, w_up, w_down, row_w[:, None])

    pos = pos.reshape(T, TOP_K)
    out = x2 + y[pos[:, 0]].astype(F32) + y[pos[:, 1]].astype(F32)
    return out.reshape(nb, L, d)


def encoder_trunk(x, norm_mix, norm_ffn, norm_final, mla_p, hy_p, moe_p):
    for i in range(DEPTH):
        j = i // 2
        if i % 2 == 0:
            x = mla_block(x, norm_mix[i], *[p[j] for p in mla_p])
        else:
            x = hyena_block(x, norm_mix[i], *[p[j] for p in hy_p])
        x = moe_block(x, norm_ffn[i], *[p[i] for p in moe_p])
    return rms_norm_pallas(x, norm_final)


def kernel(x_prompt, x_sample, norm_mix, norm_ffn, norm_final, mla_w_in, mla_q_norm, mla_w_q_up, mla_kv_norm, mla_w_kv_up, mla_w_out, hy_w_in, hy_conv_w, hy_conv_b, hy_f_w1, hy_f_b1, hy_f_freq1, hy_f_w2, hy_f_b2, hy_f_freq2, hy_f_w3, hy_f_b3, hy_skip, hy_w_out, moe_w_group, moe_b_group, moe_w_expert, moe_b_expert, moe_w_gate, moe_w_up, moe_w_down):
    mla_p = (mla_w_in, mla_q_norm, mla_w_q_up, mla_kv_norm, mla_w_kv_up, mla_w_out)
    hy_p = (hy_w_in, hy_conv_w, hy_conv_b, hy_f_w1, hy_f_b1, hy_f_freq1, hy_f_w2, hy_f_b2,
            hy_f_freq2, hy_f_w3, hy_f_b3, hy_skip, hy_w_out)
    moe_p = (moe_w_group, moe_b_group, moe_w_expert, moe_b_expert,
             moe_w_gate.astype(BF16), moe_w_up.astype(BF16), moe_w_down.astype(BF16))
    y_prompt = encoder_trunk(x_prompt, norm_mix, norm_ffn, norm_final, mla_p, hy_p, moe_p)
    y_sample = encoder_trunk(x_sample, norm_mix, norm_ffn, norm_final, mla_p, hy_p, moe_p)
    return (y_prompt, y_sample)
```

```python
import functools
import math
import jax
import jax.numpy as jnp
from jax import lax
from jax.experimental import pallas as pl
from jax.experimental.pallas import tpu as pltpu

D_MODEL = 2048
DEPTH = 2
N_HEADS = 16
Q_RANK = 512
KV_RANK = 512
QK_NOPE = 128
QK_ROPE = 64
V_DIM = 128
ROPE_THETA = 10000.0
ATTN_SCALE = 1.0 / math.sqrt(QK_NOPE + QK_ROPE)
HY_ORDER = 2
HY_EMB = 33
HY_BANDS = (HY_EMB - 1) // 2
HY_HIDDEN = 64
HY_FAST_DECAY = 0.3
HY_SLOW_DECAY = 1.5
HY_TARGET = 1e-2
N_GROUPS = 8
EXPERTS_PER_GROUP = 8
N_EXPERTS = N_GROUPS * EXPERTS_PER_GROUP
TOP_K = 2
D_EXPERT = 512
ROW_BLOCK = 128
NORM_EPS = 1e-6

F32 = jnp.float32
BF16 = jnp.bfloat16
LANES = 128
HEAD_W = 2 * LANES
VMEM_LIMIT = 56 * 1024 * 1024
FFT_GROUP = 16


def _row_tile(n, want):
    t = min(n, want)
    while n % t:
        t //= 2
    return t


def _rms(x, g):
    return x * lax.rsqrt(jnp.mean(x * x, axis=-1, keepdims=True) + NORM_EPS) * g


def _norm_mm_kernel(x_ref, g_ref, w_ref, o_ref, h_scr):
    @pl.when(pl.program_id(1) == 0)
    def _():
        h_scr[...] = _rms(x_ref[...].astype(F32), g_ref[...]).astype(BF16)

    o_ref[...] = jnp.dot(h_scr[...], w_ref[...], preferred_element_type=F32).astype(o_ref.dtype)


def norm_matmul(x, g, w, out_dtype=F32, tm=512, tn=1024):
    m, k = x.shape
    n = w.shape[1]
    tm = _row_tile(m, tm)
    tn = n if n <= 2048 else _row_tile(n, tn)
    return pl.pallas_call(
        _norm_mm_kernel,
        grid=(m // tm, n // tn),
        in_specs=[pl.BlockSpec((tm, k), lambda i, j: (i, 0)),
                  pl.BlockSpec((1, k), lambda i, j: (0, 0)),
                  pl.BlockSpec((k, tn), lambda i, j: (0, j))],
        out_specs=pl.BlockSpec((tm, tn), lambda i, j: (i, j)),
        out_shape=jax.ShapeDtypeStruct((m, n), out_dtype),
        scratch_shapes=[pltpu.VMEM((tm, k), BF16)],
        compiler_params=pltpu.CompilerParams(
            dimension_semantics=("parallel", "arbitrary"), vmem_limit_bytes=VMEM_LIMIT),
        name="norm_matmul",
    )(x, g.reshape(1, k).astype(F32), w)


def _mm_res_kernel(a_ref, w_ref, r_ref, o_ref):
    acc = jnp.dot(a_ref[...].astype(BF16), w_ref[...], preferred_element_type=F32)
    o_ref[...] = (r_ref[...].astype(F32) + acc).astype(o_ref.dtype)


def matmul_residual(a, w, r, tm=512, tn=1024):
    m, k = a.shape
    n = w.shape[1]
    tm = _row_tile(m, tm)
    tn = _row_tile(n, tn)
    return pl.pallas_call(
        _mm_res_kernel,
        grid=(m // tm, n // tn),
        in_specs=[pl.BlockSpec((tm, k), lambda i, j: (i, 0)),
                  pl.BlockSpec((k, tn), lambda i, j: (0, j)),
                  pl.BlockSpec((tm, tn), lambda i, j: (i, j))],
        out_specs=pl.BlockSpec((tm, tn), lambda i, j: (i, j)),
        out_shape=jax.ShapeDtypeStruct((m, n), r.dtype),
        compiler_params=pltpu.CompilerParams(
            dimension_semantics=("parallel", "parallel"), vmem_limit_bytes=VMEM_LIMIT),
        name="matmul_residual",
    )(a, w, r)


def _rms_kernel(x_ref, g_ref, o_ref):
    o_ref[...] = _rms(x_ref[...].astype(F32), g_ref[...]).astype(o_ref.dtype)


def rms_norm_pallas(x, g, rows=512):
    shp = x.shape
    d = shp[-1]
    x2 = x.reshape(-1, d)
    t = x2.shape[0]
    rows = _row_tile(t, rows)
    out = pl.pallas_call(
        _rms_kernel,
        grid=(t // rows,),
        in_specs=[pl.BlockSpec((rows, d), lambda i: (i, 0)),
                  pl.BlockSpec((1, d), lambda i: (0, 0))],
        out_specs=pl.BlockSpec((rows, d), lambda i: (i, 0)),
        out_shape=jax.ShapeDtypeStruct((t, d), x.dtype),
        compiler_params=pltpu.CompilerParams(dimension_semantics=("parallel",)),
        name="rms_norm",
    )(x2, g.reshape(1, d).astype(F32))
    return out.reshape(shp)


def _mla_qkv_kernel(c_ref, qg_ref, kvg_ref, wq_ref, wkv_ref, cq_ref, sq_ref, ck_ref, sk_ref,
                    q_ref, k_ref, v_ref, *, n_heads, q_rank, kv_rank):
    c = c_ref[...]
    cqn = _rms(c[:, :q_rank], qg_ref[...]).astype(BF16)
    ckvn = _rms(c[:, q_rank:q_rank + kv_rank], kvg_ref[...]).astype(BF16)
    kr = c[:, q_rank + kv_rank:]
    k_hi = (kr * ck_ref[...] + pltpu.roll(kr, LANES // 2, axis=1) * sk_ref[...]).astype(BF16)
    tr = c.shape[0]
    lane = lax.broadcasted_iota(jnp.int32, (tr, LANES), 1)
    v_hi = jnp.where(lane == 0, 1.0, 0.0).astype(BF16)
    cq = cq_ref[...]
    sq = sq_ref[...]
    qw = 3 * LANES
    for h in range(n_heads):
        a = jnp.dot(cqn, wq_ref[:, h * qw:(h + 1) * qw], preferred_element_type=F32)
        q_ref[h, :, :LANES] = (a[:, :LANES] * ATTN_SCALE).astype(BF16)
        q_ref[h, :, LANES:] = (a[:, LANES:2 * LANES] * cq + a[:, 2 * LANES:] * sq).astype(BF16)
        kv = jnp.dot(ckvn, wkv_ref[:, h * HEAD_W:(h + 1) * HEAD_W], preferred_element_type=F32)
        k_ref[h, :, :LANES] = kv[:, :LANES].astype(BF16)
        k_ref[h, :, LANES:] = k_hi
        v_ref[h, :, :LANES] = kv[:, LANES:].astype(BF16)
        v_ref[h, :, LANES:] = v_hi


def _attn_kernel(q_ref, k_ref, v_ref, o_ref, *, tk, unroll):
    q = q_ref[...]
    tq = q.shape[0]
    nk = k_ref.shape[0] // tk

    def body(j, carry):
        m, acc = carry
        off = pl.multiple_of(j * tk, tk)
        s = lax.dot_general(q, k_ref[pl.ds(off, tk), :], (((1,), (1,)), ((), ())),
                            preferred_element_type=F32)
        m_new = jnp.maximum(m, jnp.max(s, axis=1, keepdims=True))
        p = jnp.exp((s - m_new).astype(BF16))
        acc = acc * jnp.exp(m - m_new) + jnp.dot(p, v_ref[pl.ds(off, tk), :],
                                                 preferred_element_type=F32)
        return m_new, acc

    m0 = jnp.full((tq, 1), -1e30, F32)
    acc0 = jnp.zeros((tq, HEAD_W), F32)
    _, acc = lax.fori_loop(0, nk, body, (m0, acc0), unroll=min(unroll, nk))
    o_ref[...] = (acc[:, :LANES] / acc[:, LANES:LANES + 1]).astype(o_ref.dtype)


def mla_attention(q, k, v, tq=512, tk=1024, unroll=16):
    b, nh, L, _ = q.shape
    tq = _row_tile(L, tq)
    tk = _row_tile(L, tk)
    kv_spec = pl.BlockSpec((None, None, L, HEAD_W), lambda bi, h, i: (bi, h, 0, 0))
    return pl.pallas_call(
        functools.partial(_attn_kernel, tk=tk, unroll=unroll),
        grid=(b, nh, L // tq),
        in_specs=[pl.BlockSpec((None, None, tq, HEAD_W), lambda bi, h, i: (bi, h, i, 0)),
                  kv_spec, kv_spec],
        out_specs=pl.BlockSpec((None, tq, V_DIM), lambda bi, h, i: (bi, i, h)),
        out_shape=jax.ShapeDtypeStruct((b, L, nh * V_DIM), BF16),
        compiler_params=pltpu.CompilerParams(
            dimension_semantics=("parallel", "parallel", "arbitrary"),
            vmem_limit_bytes=VMEM_LIMIT),
        name="mla_attention",
    )(q, k, v)


def _rot_cols(w):
    half = w.shape[-1] // 2
    return jnp.concatenate([-w[..., half:], w[..., :half]], axis=-1)


def mla_block(x, norm_g, w_in, q_norm, w_q_up, kv_norm, w_kv_up, w_out, tq=512, tk=1024):
    assert QK_NOPE == LANES and V_DIM == LANES and QK_ROPE == LANES // 2
    b, L, d = x.shape
    t = b * L
    nh = N_HEADS
    x2 = x.reshape(t, d)
    rk = Q_RANK + KV_RANK
    w_in_ext = jnp.concatenate([w_in, _rot_cols(w_in[:, rk:])], axis=1).astype(BF16)
    c = norm_matmul(x2, norm_g, w_in_ext)

    wq = w_q_up.reshape(Q_RANK, nh, QK_NOPE + QK_ROPE)
    rope_w = wq[..., QK_NOPE:]
    z = jnp.zeros_like(rope_w)
    wq3 = jnp.concatenate([wq[..., :QK_NOPE], rope_w, z, _rot_cols(rope_w), z], axis=-1)
    wq3 = wq3.reshape(Q_RANK, nh * 3 * LANES).astype(BF16)
    wkv = w_kv_up.astype(BF16)

    inv = 1.0 / (ROPE_THETA ** (jnp.arange(0, QK_ROPE, 2, dtype=F32) / QK_ROPE))
    ang = jnp.arange(L, dtype=F32)[:, None] * inv[None, :]
    ang = jnp.concatenate([ang, ang], axis=-1)
    zpad = jnp.zeros((L, LANES - QK_ROPE), F32)
    cos_k = jnp.concatenate([jnp.cos(ang), zpad], axis=1)
    sin_k = jnp.concatenate([jnp.sin(ang), zpad], axis=1)
    cos_q = cos_k * ATTN_SCALE
    sin_q = sin_k * ATTN_SCALE

    tr = _row_tile(L, 256)
    nl = L // tr
    cw = c.shape[1]
    tab = pl.BlockSpec((tr, LANES), lambda i: (i % nl, 0))
    slab = pl.BlockSpec((None, nh, tr, HEAD_W), lambda i: (i // nl, 0, i % nl, 0))
    slab_shape = jax.ShapeDtypeStruct((b, nh, L, HEAD_W), BF16)
    q, k, v = pl.pallas_call(
        functools.partial(_mla_qkv_kernel, n_heads=nh, q_rank=Q_RANK, kv_rank=KV_RANK),
        grid=(t // tr,),
        in_specs=[pl.BlockSpec((tr, cw), lambda i: (i, 0)),
                  pl.BlockSpec((1, Q_RANK), lambda i: (0, 0)),
                  pl.BlockSpec((1, KV_RANK), lambda i: (0, 0)),
                  pl.BlockSpec(wq3.shape, lambda i: (0, 0)),
                  pl.BlockSpec(wkv.shape, lambda i: (0, 0)),
                  tab, tab, tab, tab],
        out_specs=[slab, slab, slab],
        out_shape=[slab_shape, slab_shape, slab_shape],
        compiler_params=pltpu.CompilerParams(
            dimension_semantics=("parallel",), vmem_limit_bytes=VMEM_LIMIT),
        name="mla_qkv",
    )(c, q_norm.reshape(1, -1).astype(F32), kv_norm.reshape(1, -1).astype(F32), wq3, wkv,
      cos_q, sin_q, cos_k, sin_k)

    o = mla_attention(q, k, v, tq=tq, tk=tk)

    out = matmul_residual(o.reshape(t, nh * V_DIM), w_out.astype(BF16), x2)
    return out.reshape(b, L, d)


def _fft_split(L):
    n = 2 * L
    n2 = min(256, n // 16)
    return n // n2, n2


def _fft_tables(L):
    n1, n2 = _fft_split(L)
    n = n1 * n2
    n1h = n1 // 2
    k1n = n1 // 2 + 1
    k1p = min(n1, -(-k1n // 8) * 8)
    g = FFT_GROUP
    nb = n2 // g
    k1 = jnp.arange(k1p, dtype=jnp.int32)
    m1 = jnp.arange(n1h, dtype=jnp.int32)
    m2 = jnp.arange(n2, dtype=jnp.int32)
    idx = (m2[:, None, None] * k1[None, :, None] + n2 * k1[None, :, None] * m1[None, None, :]) % n
    th = idx.astype(F32) * (2.0 * math.pi / n)
    live = (k1 < k1n).astype(F32)
    wgt = jnp.where((k1 == 0) | (k1 == n1 // 2), 1.0, 2.0) * live / n1
    c, s = jnp.cos(th), jnp.sin(th)
    eye = jnp.eye(g, dtype=F32)
    fa = (jnp.stack([c, -s], axis=1) * live[None, None, :, None]).reshape(nb, g, 2, k1p, n1h)
    fa = jnp.einsum('bsckn,ts->bcktns', fa, eye).reshape(nb, 2 * k1p * g, n1h * g).astype(BF16)
    gc = (jnp.stack([c, -s], axis=1) * wgt[None, None, :, None]).reshape(nb, g, 2, k1p, n1h)
    gc = jnp.einsum('bsckn,ts->bntcks', gc, eye).reshape(nb, n1h * g, 2 * k1p * g).astype(BF16)
    idx2 = (m2[:, None] * m2[None, :]) % n2
    th2 = idx2.astype(F32) * (2.0 * math.pi / n2)
    c2, s2 = jnp.cos(th2), jnp.sin(th2)
    mf = jnp.concatenate([jnp.concatenate([c2, s2], 1), jnp.concatenate([-s2, c2], 1)], 0).astype(BF16)
    mi = (jnp.concatenate([jnp.concatenate([c2, -s2], 1), jnp.concatenate([s2, c2], 1)], 0) / n2).astype(BF16)
    return fa, gc, mf, mi


def _fft_a_kernel(x_ref, fa_ref, y_ref):
    n1h, g, tc = x_ref.shape
    y = jnp.dot(fa_ref[...], x_ref[...].reshape(n1h * g, tc), preferred_element_type=F32)
    y_ref[...] = y.reshape(y_ref.shape).astype(y_ref.dtype)


def fft_stage_a(x, fa, tc=512):
    s, L, d = x.shape
    nb, rows, cols = fa.shape
    g = FFT_GROUP
    n1h, k1p, n2 = cols // g, rows // (2 * g), nb * g
    tc = _row_tile(d, tc)
    return pl.pallas_call(
        _fft_a_kernel,
        grid=(s, nb, d // tc),
        in_specs=[pl.BlockSpec((None, n1h, g, tc), lambda b, j, c: (b, 0, j, c)),
                  pl.BlockSpec((None, rows, cols), lambda b, j, c: (j, 0, 0))],
        out_specs=pl.BlockSpec((None, 2, k1p, g, tc), lambda b, j, c: (b, 0, 0, j, c)),
        out_shape=jax.ShapeDtypeStruct((s, 2, k1p, n2, d), BF16),
        compiler_params=pltpu.CompilerParams(
            dimension_semantics=("parallel", "parallel", "parallel"), vmem_limit_bytes=VMEM_LIMIT),
        name="fft_stage_a",
    )(x.reshape(s, n1h, n2, d), fa)


def _fft_b_kernel(y_ref, h_ref, mf_ref, mi_ref, u_ref):
    n2 = y_ref.shape[1]
    tc = y_ref.shape[2]
    z = jnp.dot(mf_ref[...], y_ref[...].reshape(2 * n2, tc), preferred_element_type=F32)
    zr, zi = z[:n2], z[n2:]
    hr, hi = h_ref[0], h_ref[1]
    w = jnp.concatenate([zr * hr - zi * hi, zr * hi + zi * hr], axis=0).astype(BF16)
    u = jnp.dot(mi_ref[...], w, preferred_element_type=F32)
    u_ref[0] = u[:n2].astype(u_ref.dtype)
    u_ref[1] = u[n2:].astype(u_ref.dtype)


def fft_stage_b(y, spec, order, mf, mi, tc=1024):
    s, _, k1p, n2, d = y.shape
    tc = _row_tile(d, tc)
    blk = pl.BlockSpec((None, 2, None, n2, tc), lambda k, j, b: (b, 0, k, 0, j))
    return pl.pallas_call(
        _fft_b_kernel,
        grid=(k1p, d // tc, s),
        in_specs=[blk,
                  pl.BlockSpec((None, 2, None, n2, tc), lambda k, j, b: (order, 0, k, 0, j)),
                  pl.BlockSpec(mf.shape, lambda k, j, b: (0, 0)),
                  pl.BlockSpec(mi.shape, lambda k, j, b: (0, 0))],
        out_specs=blk,
        out_shape=jax.ShapeDtypeStruct(y.shape, BF16),
        compiler_params=pltpu.CompilerParams(
            dimension_semantics=("parallel", "parallel", "arbitrary"), vmem_limit_bytes=VMEM_LIMIT),
        name="fft_stage_b",
    )(y, spec, mf, mi)


def _fft_c_kernel(u_ref, gc_ref, v_ref, g_ref, sk_ref, o_ref):
    _, k1p, g, tc = u_ref.shape
    rows = v_ref.shape[0] * g
    y = jnp.dot(gc_ref[...], u_ref[...].reshape(2 * k1p * g, tc), preferred_element_type=F32)
    v = v_ref[...].reshape(rows, tc).astype(F32)
    gate = g_ref[...].reshape(rows, tc).astype(F32)
    o_ref[...] = (gate * (y + v * sk_ref[...])).astype(o_ref.dtype).reshape(o_ref.shape)


def fft_stage_c(u, gc, v, gate, skip, tc=512):
    s, _, k1p, n2, d = u.shape
    nb, rows, cols = gc.shape
    g = FFT_GROUP
    n1h = rows // g
    L = n1h * n2
    tc = _row_tile(d, tc)
    row = pl.BlockSpec((None, n1h, g, tc), lambda b, j, c: (b, 0, j, c))
    out = pl.pallas_call(
        _fft_c_kernel,
        grid=(s, nb, d // tc),
        in_specs=[pl.BlockSpec((None, 2, k1p, g, tc), lambda b, j, c: (b, 0, 0, j, c)),
                  pl.BlockSpec((None, rows, cols), lambda b, j, c: (j, 0, 0)),
                  row, row,
                  pl.BlockSpec((1, tc), lambda b, j, c: (0, c))],
        out_specs=row,
        out_shape=jax.ShapeDtypeStruct((s, n1h, n2, d), BF16),
        compiler_params=pltpu.CompilerParams(
            dimension_semantics=("parallel", "parallel", "parallel"), vmem_limit_bytes=VMEM_LIMIT),
        name="fft_stage_c",
    )(u, gc, v.reshape(s, n1h, n2, d), gate.reshape(s, n1h, n2, d), skip.reshape(1, d).astype(F32))
    return out.reshape(s, L, d)


def _filter_kernel(ft_ref, w1_ref, b1_ref, f1_ref, w2_ref, b2_ref, f2_ref, w3_ref, b3_ref, dl_ref,
                   taps_ref, sum_ref, *, seq_len):
    i = pl.program_id(1)
    tl = ft_ref.shape[0]
    a = jnp.sin(f1_ref[...] * (jnp.dot(ft_ref[...], w1_ref[...], preferred_element_type=F32) + b1_ref[...]))
    a = jnp.sin(f2_ref[...] * (jnp.dot(a, w2_ref[...], preferred_element_type=F32) + b2_ref[...]))
    f = jnp.dot(a, w3_ref[...], preferred_element_type=F32) + b3_ref[...]
    pos = (lax.broadcasted_iota(jnp.int32, (tl, 1), 0) + i * tl).astype(F32)
    t = pos * (1.0 / max(seq_len - 1, 1))
    f = f * jnp.exp(-t * dl_ref[...])

    @pl.when(i == 0)
    def _():
        sum_ref[...] = jnp.zeros_like(sum_ref)

    sum_ref[...] += jnp.sum(jnp.abs(f), axis=0, keepdims=True)
    is_bwd = pl.program_id(0) % 2 == 1
    f = jnp.where(jnp.logical_and(is_bwd, pos == 0.0), 0.0, f)
    taps_ref[...] = f.astype(taps_ref.dtype)


def hyena_filter_taps(L, d, f_w1, f_b1, f_fr1, f_w2, f_b2, f_fr2, f_w3, f_b3, tl=512):
    hp = LANES
    t = jnp.linspace(0.0, 1.0, L, dtype=F32)[:, None]
    w = (2.0 * math.pi / L) * jnp.arange(L, dtype=F32)[:, None]
    bands = jnp.linspace(1e-4, HY_BANDS - 1, HY_BANDS, dtype=F32)[None, :]
    feats = jnp.concatenate([t, jnp.cos(bands * w), -jnp.sin(bands * w),
                             jnp.zeros((L, hp - HY_EMB), F32)], axis=-1)

    def pad2(m, r, c):
        return jnp.pad(m.astype(F32), ((0, r - m.shape[0]), (0, c - m.shape[1])))

    w1 = pad2(f_w1, hp, hp)
    w2 = pad2(f_w2, hp, hp)
    w3 = pad2(f_w3, hp, f_w3.shape[1])
    b1, fr1 = pad2(f_b1[None], 1, hp), pad2(f_fr1[None], 1, hp)
    b2, fr2 = pad2(f_b2[None], 1, hp), pad2(f_fr2[None], 1, hp)
    max_decay = math.log(HY_TARGET) / HY_FAST_DECAY
    min_decay = math.log(HY_TARGET) / HY_SLOW_DECAY
    deltas = jnp.abs(jnp.linspace(min_decay, max_decay, d, dtype=F32))[None, :]
    ns = HY_ORDER * 2
    tl = _row_tile(L, tl)
    vec = lambda n: pl.BlockSpec((1, n), lambda s, i: (0, 0))
    mat = lambda r, c: pl.BlockSpec((r, c), lambda s, i: (0, 0))
    return pl.pallas_call(
        functools.partial(_filter_kernel, seq_len=L),
        grid=(ns, L // tl),
        in_specs=[pl.BlockSpec((tl, hp), lambda s, i: (i, 0)),
                  mat(hp, hp), vec(hp), vec(hp), mat(hp, hp), vec(hp), vec(hp),
                  pl.BlockSpec((hp, d), lambda s, i: (0, s)),
                  pl.BlockSpec((1, d), lambda s, i: (0, s)),
                  vec(d)],
        out_specs=[pl.BlockSpec((None, tl, d), lambda s, i: (s, i, 0)),
                   pl.BlockSpec((None, 1, d), lambda s, i: (s, 0, 0))],
        out_shape=[jax.ShapeDtypeStruct((ns, L, d), BF16), jax.ShapeDtypeStruct((ns, 1, d), F32)],
        compiler_params=pltpu.CompilerParams(dimension_semantics=("parallel", "arbitrary")),
        name="hyena_filter_taps",
    )(feats, w1, b1, fr1, w2, b2, fr2, w3, f_b3[None].astype(F32), deltas)


def _fft_bf_kernel(yf_ref, yb_ref, sf_ref, sb_ref, mf_ref, h_ref):
    n2 = yf_ref.shape[1]
    tc = yf_ref.shape[2]
    zf = jnp.dot(mf_ref[...], yf_ref[...].reshape(2 * n2, tc), preferred_element_type=F32)
    zb = jnp.dot(mf_ref[...], yb_ref[...].reshape(2 * n2, tc), preferred_element_type=F32)
    sf = 1.0 / (sf_ref[...] + 1e-6)
    sb = 1.0 / (sb_ref[...] + 1e-6)
    h_ref[0] = zf[:n2] * sf + zb[:n2] * sb
    h_ref[1] = zf[n2:] * sf - zb[n2:] * sb


def hyena_spectrum(L, d, fa, mf, *filter_params, tc=1024):
    taps, sums = hyena_filter_taps(L, d, *filter_params)
    yv = fft_stage_a(taps, fa)
    _, _, n1, n2, _ = yv.shape
    tc = _row_tile(d, tc)
    side = lambda sd: pl.BlockSpec((None, 2, None, n2, tc), lambda o, k, j: (2 * o + sd, 0, k, 0, j))
    norm = lambda sd: pl.BlockSpec((None, 1, tc), lambda o, k, j: (2 * o + sd, 0, j))
    return pl.pallas_call(
        _fft_bf_kernel,
        grid=(HY_ORDER, n1, d // tc),
        in_specs=[side(0), side(1), norm(0), norm(1), pl.BlockSpec(mf.shape, lambda o, k, j: (0, 0))],
        out_specs=pl.BlockSpec((None, 2, None, n2, tc), lambda o, k, j: (o, 0, k, 0, j)),
        out_shape=jax.ShapeDtypeStruct((HY_ORDER, 2, n1, n2, d), F32),
        compiler_params=pltpu.CompilerParams(
            dimension_semantics=("parallel", "parallel", "parallel"), vmem_limit_bytes=VMEM_LIMIT),
        name="fft_stage_b_filter",
    )(yv, yv, sums, sums, mf)


def _shortconv_kernel(u_ref, up_ref, un_ref, w_ref, b_ref, v_ref, g1_ref, g2_ref):
    i = pl.program_id(1)
    tr = u_ref.shape[0]
    d = v_ref.shape[1]
    hr = up_ref.shape[0]
    first = i == 0
    last = i == pl.num_programs(1) - 1
    for j, o_ref in enumerate((v_ref, g1_ref, g2_ref)):
        cols = slice(j * d, (j + 1) * d)
        u = u_ref[:, cols].astype(F32)
        row = lax.broadcasted_iota(jnp.int32, u.shape, 0)
        prev_row = jnp.where(first, 0.0, up_ref[hr - 1:hr, cols].astype(F32))
        next_row = jnp.where(last, 0.0, un_ref[0:1, cols].astype(F32))
        um = jnp.where(row == 0, prev_row, pltpu.roll(u, 1, axis=0))
        up = jnp.where(row == tr - 1, next_row, pltpu.roll(u, tr - 1, axis=0))
        w = w_ref[:, cols]
        o_ref[...] = (um * w[0:1] + u * w[1:2] + up * w[2:3] + b_ref[:, cols]).astype(o_ref.dtype)


def hyena_shortconv(u, conv_w, conv_b, tr=256):
    b, L, d3 = u.shape
    d = d3 // 3
    tr = _row_tile(L, tr)
    hr = 16
    hb = tr // hr
    nhb = L // hr
    out = pl.BlockSpec((None, tr, d), lambda bi, i: (bi, i, 0))
    shape = jax.ShapeDtypeStruct((b, L, d), BF16)
    return pl.pallas_call(
        _shortconv_kernel,
        grid=(b, L // tr),
        in_specs=[pl.BlockSpec((None, tr, d3), lambda bi, i: (bi, i, 0)),
                  pl.BlockSpec((None, hr, d3), lambda bi, i: (bi, jnp.maximum(i * hb - 1, 0), 0)),
                  pl.BlockSpec((None, hr, d3), lambda bi, i: (bi, jnp.minimum((i + 1) * hb, nhb - 1), 0)),
                  pl.BlockSpec((3, d3), lambda bi, i: (0, 0)),
                  pl.BlockSpec((1, d3), lambda bi, i: (0, 0))],
        out_specs=[out, out, out],
        out_shape=[shape, shape, shape],
        compiler_params=pltpu.CompilerParams(dimension_semantics=("parallel", "parallel")),
        name="hyena_shortconv",
    )(u, u, u, conv_w.astype(F32), conv_b[None].astype(F32))


def hyena_block(x, norm_g, w_in, conv_w, conv_b, f_w1, f_b1, f_fr1, f_w2, f_b2, f_fr2, f_w3, f_b3,
                skip, w_out):
    b, L, d = x.shape
    t = b * L
    x2 = x.reshape(t, d)
    u = norm_matmul(x2, norm_g, w_in.astype(BF16), out_dtype=BF16).reshape(b, L, 3 * d)
    v, g1, g2 = hyena_shortconv(u, conv_w, conv_b)
    fa, gc, mf, mi = _fft_tables(L)
    spec = hyena_spectrum(L, d, fa, mf, f_w1, f_b1, f_fr1, f_w2, f_b2, f_fr2, f_w3, f_b3)
    z = fft_stage_c(fft_stage_b(fft_stage_a(v, fa), spec, 0, mf, mi), gc, v, g1, skip[0])
    y = fft_stage_c(fft_stage_b(fft_stage_a(z, fa), spec, 1, mf, mi), gc, z, g2, skip[1])
    out = matmul_residual(y.reshape(t, d), w_out.astype(BF16), x2)
    return out.reshape(b, L, d)


def _router_kernel(x_ref, g_ref, w_ref, lg_ref, h_ref):
    h = _rms(x_ref[...].astype(F32), g_ref[...]).astype(BF16)
    h_ref[...] = h
    lg_ref[...] = jnp.dot(h, w_ref[...], preferred_element_type=F32)


def _expert_kernel(be_ref, nu_ref, x_ref, wg_ref, wu_ref, wd_ref, rw_ref, o_ref, wg_s, wu_s, wd_s):
    i = pl.program_id(0)
    used = i < nu_ref[0]

    @pl.when(jnp.logical_or(i == 0, be_ref[i] != be_ref[jnp.maximum(i - 1, 0)]))
    def _():
        wg_s[...] = wg_ref[...].astype(BF16)
        wu_s[...] = wu_ref[...].astype(BF16)
        wd_s[...] = wd_ref[...].astype(BF16)

    @pl.when(used)
    def _():
        x = x_ref[...]
        g = jnp.dot(x, wg_s[...], preferred_element_type=F32)
        u = jnp.dot(x, wu_s[...], preferred_element_type=F32)
        a = (g * jax.nn.sigmoid(g) * u).astype(BF16)
        y = jnp.dot(a, wd_s[...], preferred_element_type=F32)
        o_ref[...] = (y * rw_ref[...]).astype(o_ref.dtype)

    @pl.when(jnp.logical_not(used))
    def _():
        o_ref[...] = jnp.zeros_like(o_ref)


def moe_block(x, norm_g, w_rg, b_rg, w_re, b_re, w_gate, w_up, w_down, tm=256):
    nb, L, d = x.shape
    T = nb * L
    ne = N_EXPERTS
    x2 = x.reshape(T, d)
    nr = N_GROUPS + ne
    w_r = jnp.concatenate([w_rg, w_re, jnp.zeros((d, LANES - nr), w_rg.dtype)], axis=1).astype(BF16)
    tr = _row_tile(T, 512)
    logits, h = pl.pallas_call(
        _router_kernel,
        grid=(T // tr,),
        in_specs=[pl.BlockSpec((tr, d), lambda i: (i, 0)),
                  pl.BlockSpec((1, d), lambda i: (0, 0)),
                  pl.BlockSpec((d, LANES), lambda i: (0, 0))],
        out_specs=[pl.BlockSpec((tr, LANES), lambda i: (i, 0)),
                   pl.BlockSpec((tr, d), lambda i: (i, 0))],
        out_shape=[jax.ShapeDtypeStruct((T, LANES), F32), jax.ShapeDtypeStruct((T, d), BF16)],
        compiler_params=pltpu.CompilerParams(dimension_semantics=("parallel",)),
        name="moe_router",
    )(x2, norm_g.reshape(1, d).astype(F32), w_r)

    lg = logits[:, :N_GROUPS] + b_rg.astype(F32)
    pg = jax.nn.softmax(lg, axis=-1)
    gsel = jnp.argmax(lg, axis=-1).astype(jnp.int32)
    pg_sel = jnp.take_along_axis(pg, gsel[:, None], axis=1)
    le = (logits[:, N_GROUPS:nr] + b_re.astype(F32)).reshape(T, N_GROUPS, EXPERTS_PER_GROUP)
    le = jnp.take_along_axis(le, gsel[:, None, None], axis=1)[:, 0]
    pe = jax.nn.softmax(le, axis=-1)
    top_p, top_i = lax.top_k(pe, TOP_K)
    gate = pg_sel * top_p / jnp.sum(top_p, axis=-1, keepdims=True)
    eid = (gsel[:, None] * EXPERTS_PER_GROUP + top_i.astype(jnp.int32)).reshape(-1)
    gw = gate.reshape(-1)

    na = T * TOP_K
    order = jnp.argsort(eid)
    se = eid[order]
    counts = jnp.bincount(eid, length=ne).astype(jnp.int32)
    start = jnp.cumsum(counts) - counts
    pcounts = (counts + tm - 1) // tm * tm
    pend = jnp.cumsum(pcounts)
    pstart = pend - pcounts
    n_rows = na + ne * tm
    n_blk = n_rows // tm
    dest = pstart[se] + jnp.arange(na, dtype=jnp.int32) - start[se]
    row_tok = jnp.zeros((n_rows,), jnp.int32).at[dest].set(order // TOP_K)
    row_w = jnp.zeros((n_rows,), F32).at[dest].set(gw[order])
    pos = jnp.zeros((na,), jnp.int32).at[order].set(dest)
    blk_e = jnp.minimum(jnp.searchsorted(pend, jnp.arange(n_blk, dtype=jnp.int32) * tm, side='right'),
                        ne - 1).astype(jnp.int32)
    n_used = (pend[-1:] // tm).astype(jnp.int32)

    xs = h[row_tok]
    de = w_gate.shape[-1]
    y = pl.pallas_call(
        _expert_kernel,
        grid_spec=pltpu.PrefetchScalarGridSpec(
            num_scalar_prefetch=2,
            grid=(n_blk,),
            in_specs=[pl.BlockSpec((tm, d), lambda i, be, nu: (i, 0)),
                      pl.BlockSpec((None, d, de), lambda i, be, nu: (be[i], 0, 0)),
                      pl.BlockSpec((None, d, de), lambda i, be, nu: (be[i], 0, 0)),
                      pl.BlockSpec((None, de, d), lambda i, be, nu: (be[i], 0, 0)),
                      pl.BlockSpec((tm, 1), lambda i, be, nu: (i, 0))],
            out_specs=pl.BlockSpec((tm, d), lambda i, be, nu: (i, 0)),
            scratch_shapes=[pltpu.VMEM((d, de), BF16), pltpu.VMEM((d, de), BF16),
                            pltpu.VMEM((de, d), BF16)]),
        out_shape=jax.ShapeDtypeStruct((n_rows, d), BF16),
        compiler_params=pltpu.CompilerParams(
            dimension_semantics=("arbitrary",), vmem_limit_bytes=VMEM_LIMIT),
        name="moe_experts",
    )(blk_e, n_used, xs, w_gate, w_up, w_down, row_w[:, None])

    pos = pos.reshape(T, TOP_K)
    out = x2 + y[pos[:, 0]].astype(F32) + y[pos[:, 1]].astype(F32)
    return out.reshape(nb, L, d)


def encoder_trunk(x, norm_mix, norm_ffn, norm_final, mla_p, hy_p, moe_p):
    for i in range(DEPTH):
        j = i // 2
        if i % 2 == 0:
            x = mla_block(x, norm_mix[i], *[p[j] for p in mla_p])
        else:
            x = hyena_block(x, norm_mix[i], *[p[j] for p in hy_p])
        x = moe_block(x, norm_ffn[i], *[p[i] for p in moe_p])
    return rms_norm_pallas(x, norm_final)


def kernel(x_prompt, x_sample, norm_mix, norm_ffn, norm_final, mla_w_in, mla_q_norm, mla_w_q_up, mla_kv_norm, mla_w_kv_up, mla_w_out, hy_w_in, hy_conv_w, hy_conv_b, hy_f_w1, hy_f_b1, hy_f_freq1, hy_f_w2, hy_f_b2, hy_f_freq2, hy_f_w3, hy_f_b3, hy_skip, hy_w_out, moe_w_group, moe_b_group, moe_w_expert, moe_b_expert, moe_w_gate, moe_w_up, moe_w_down):
    mla_p = (mla_w_in, mla_q_norm, mla_w_q_up, mla_kv_norm, mla_w_kv_up, mla_w_out)
    hy_p = (hy_w_in, hy_conv_w, hy_conv_b, hy_f_w1, hy_f_b1, hy_f_freq1, hy_f_w2, hy_f_b2,
            hy_f_freq2, hy_f_w3, hy_f_b3, hy_skip, hy_w_out)
    moe_p = (moe_w_group, moe_b_group, moe_w_expert, moe_b_expert, moe_w_gate, moe_w_up, moe_w_down)
    y_prompt = encoder_trunk(x_prompt, norm_mix, norm_ffn, norm_final, mla_p, hy_p, moe_p)
    y_sample = encoder_trunk(x_sample, norm_mix, norm_ffn, norm_final, mla_p, hy_p, moe_p)
    return (y_prompt, y_sample)
```

```python
import functools
import math
import jax
import jax.numpy as jnp
from jax import lax
from jax.experimental import pallas as pl
from jax.experimental.pallas import tpu as pltpu

D_MODEL = 2048
DEPTH = 2
N_HEADS = 16
Q_RANK = 512
KV_RANK = 512
QK_NOPE = 128
QK_ROPE = 64
V_DIM = 128
ROPE_THETA = 10000.0
ATTN_SCALE = 1.0 / math.sqrt(QK_NOPE + QK_ROPE)
HY_ORDER = 2
HY_EMB = 33
HY_BANDS = (HY_EMB - 1) // 2
HY_HIDDEN = 64
HY_FAST_DECAY = 0.3
HY_SLOW_DECAY = 1.5
HY_TARGET = 1e-2
N_GROUPS = 8
EXPERTS_PER_GROUP = 8
N_EXPERTS = N_GROUPS * EXPERTS_PER_GROUP
TOP_K = 2
D_EXPERT = 512
ROW_BLOCK = 128
NORM_EPS = 1e-6

F32 = jnp.float32
BF16 = jnp.bfloat16
LANES = 128
HEAD_W = 2 * LANES
VMEM_LIMIT = 56 * 1024 * 1024
FFT_GROUP = 16


def _row_tile(n, want):
    t = min(n, want)
    while n % t:
        t //= 2
    return t


def _rms(x, g):
    return x * lax.rsqrt(jnp.mean(x * x, axis=-1, keepdims=True) + NORM_EPS) * g


def _norm_mm_kernel(x_ref, g_ref, w_ref, o_ref, h_scr):
    @pl.when(pl.program_id(1) == 0)
    def _():
        h_scr[...] = _rms(x_ref[...].astype(F32), g_ref[...]).astype(BF16)

    o_ref[...] = jnp.dot(h_scr[...], w_ref[...], preferred_element_type=F32).astype(o_ref.dtype)


def norm_matmul(x, g, w, out_dtype=F32, tm=512, tn=1024):
    m, k = x.shape
    n = w.shape[1]
    tm = _row_tile(m, tm)
    tn = n if n <= 2048 else _row_tile(n, tn)
    return pl.pallas_call(
        _norm_mm_kernel,
        grid=(m // tm, n // tn),
        in_specs=[pl.BlockSpec((tm, k), lambda i, j: (i, 0)),
                  pl.BlockSpec((1, k), lambda i, j: (0, 0)),
                  pl.BlockSpec((k, tn), lambda i, j: (0, j))],
        out_specs=pl.BlockSpec((tm, tn), lambda i, j: (i, j)),
        out_shape=jax.ShapeDtypeStruct((m, n), out_dtype),
        scratch_shapes=[pltpu.VMEM((tm, k), BF16)],
        compiler_params=pltpu.CompilerParams(
            dimension_semantics=("parallel", "arbitrary"), vmem_limit_bytes=VMEM_LIMIT),
        name="norm_matmul",
    )(x, g.reshape(1, k).astype(F32), w)


def _mm_res_kernel(a_ref, w_ref, r_ref, o_ref):
    acc = jnp.dot(a_ref[...].astype(BF16), w_ref[...], preferred_element_type=F32)
    o_ref[...] = (r_ref[...].astype(F32) + acc).astype(o_ref.dtype)


def matmul_residual(a, w, r, tm=512, tn=1024):
    m, k = a.shape
    n = w.shape[1]
    tm = _row_tile(m, tm)
    tn = _row_tile(n, tn)
    return pl.pallas_call(
        _mm_res_kernel,
        grid=(m // tm, n // tn),
        in_specs=[pl.BlockSpec((tm, k), lambda i, j: (i, 0)),
                  pl.BlockSpec((k, tn), lambda i, j: (0, j)),
                  pl.BlockSpec((tm, tn), lambda i, j: (i, j))],
        out_specs=pl.BlockSpec((tm, tn), lambda i, j: (i, j)),
        out_shape=jax.ShapeDtypeStruct((m, n), r.dtype),
        compiler_params=pltpu.CompilerParams(
            dimension_semantics=("parallel", "parallel"), vmem_limit_bytes=VMEM_LIMIT),
        name="matmul_residual",
    )(a, w, r)


def _rms_kernel(x_ref, g_ref, o_ref):
    o_ref[...] = _rms(x_ref[...].astype(F32), g_ref[...]).astype(o_ref.dtype)


def rms_norm_pallas(x, g, rows=512):
    shp = x.shape
    d = shp[-1]
    x2 = x.reshape(-1, d)
    t = x2.shape[0]
    rows = _row_tile(t, rows)
    out = pl.pallas_call(
        _rms_kernel,
        grid=(t // rows,),
        in_specs=[pl.BlockSpec((rows, d), lambda i: (i, 0)),
                  pl.BlockSpec((1, d), lambda i: (0, 0))],
        out_specs=pl.BlockSpec((rows, d), lambda i: (i, 0)),
        out_shape=jax.ShapeDtypeStruct((t, d), x.dtype),
        compiler_params=pltpu.CompilerParams(dimension_semantics=("parallel",)),
        name="rms_norm",
    )(x2, g.reshape(1, d).astype(F32))
    return out.reshape(shp)


def _mla_qkv_kernel(c_ref, qg_ref, kvg_ref, wq_ref, wkv_ref, cq_ref, sq_ref, ck_ref, sk_ref,
                    q_ref, k_ref, v_ref, *, n_heads, q_rank, kv_rank):
    c = c_ref[...]
    cqn = _rms(c[:, :q_rank], qg_ref[...]).astype(BF16)
    ckvn = _rms(c[:, q_rank:q_rank + kv_rank], kvg_ref[...]).astype(BF16)
    kr = c[:, q_rank + kv_rank:]
    k_hi = (kr * ck_ref[...] + pltpu.roll(kr, LANES // 2, axis=1) * sk_ref[...]).astype(BF16)
    tr = c.shape[0]
    lane = lax.broadcasted_iota(jnp.int32, (tr, LANES), 1)
    v_hi = jnp.where(lane == 0, 1.0, 0.0).astype(BF16)
    cq = cq_ref[...]
    sq = sq_ref[...]
    qw = 3 * LANES
    for h in range(n_heads):
        a = jnp.dot(cqn, wq_ref[:, h * qw:(h + 1) * qw], preferred_element_type=F32)
        q_ref[h, :, :LANES] = (a[:, :LANES] * ATTN_SCALE).astype(BF16)
        q_ref[h, :, LANES:] = (a[:, LANES:2 * LANES] * cq + a[:, 2 * LANES:] * sq).astype(BF16)
        kv = jnp.dot(ckvn, wkv_ref[:, h * HEAD_W:(h + 1) * HEAD_W], preferred_element_type=F32)
        k_ref[h, :, :LANES] = kv[:, :LANES].astype(BF16)
        k_ref[h, :, LANES:] = k_hi
        v_ref[h, :, :LANES] = kv[:, LANES:].astype(BF16)
        v_ref[h, :, LANES:] = v_hi


def _attn_kernel(q_ref, k_ref, v_ref, o_ref, *, tk, unroll):
    q = q_ref[...]
    tq = q.shape[0]
    nk = k_ref.shape[0] // tk

    def body(j, carry):
        m, acc = carry
        off = pl.multiple_of(j * tk, tk)
        s = lax.dot_general(q, k_ref[pl.ds(off, tk), :], (((1,), (1,)), ((), ())),
                            preferred_element_type=F32)
        m_new = jnp.maximum(m, jnp.max(s, axis=1, keepdims=True))
        p = jnp.exp((s - m_new).astype(BF16))
        acc = acc * jnp.exp(m - m_new) + jnp.dot(p, v_ref[pl.ds(off, tk), :],
                                                 preferred_element_type=F32)
        return m_new, acc

    m0 = jnp.full((tq, 1), -1e30, F32)
    acc0 = jnp.zeros((tq, HEAD_W), F32)
    _, acc = lax.fori_loop(0, nk, body, (m0, acc0), unroll=min(unroll, nk))
    o_ref[...] = (acc[:, :LANES] / acc[:, LANES:LANES + 1]).astype(o_ref.dtype)


def mla_attention(q, k, v, tq=512, tk=1024, unroll=16):
    b, nh, L, _ = q.shape
    tq = _row_tile(L, tq)
    tk = _row_tile(L, tk)
    kv_spec = pl.BlockSpec((None, None, L, HEAD_W), lambda bi, h, i: (bi, h, 0, 0))
    return pl.pallas_call(
        functools.partial(_attn_kernel, tk=tk, unroll=unroll),
        grid=(b, nh, L // tq),
        in_specs=[pl.BlockSpec((None, None, tq, HEAD_W), lambda bi, h, i: (bi, h, i, 0)),
                  kv_spec, kv_spec],
        out_specs=pl.BlockSpec((None, tq, V_DIM), lambda bi, h, i: (bi, i, h)),
        out_shape=jax.ShapeDtypeStruct((b, L, nh * V_DIM), BF16),
        compiler_params=pltpu.CompilerParams(
            dimension_semantics=("parallel", "parallel", "arbitrary"),
            vmem_limit_bytes=VMEM_LIMIT),
        name="mla_attention",
    )(q, k, v)


def _rot_cols(w):
    half = w.shape[-1] // 2
    return jnp.concatenate([-w[..., half:], w[..., :half]], axis=-1)


def mla_block(x, norm_g, w_in, q_norm, w_q_up, kv_norm, w_kv_up, w_out, tq=512, tk=1024):
    assert QK_NOPE == LANES and V_DIM == LANES and QK_ROPE == LANES // 2
    b, L, d = x.shape
    t = b * L
    nh = N_HEADS
    x2 = x.reshape(t, d)
    rk = Q_RANK + KV_RANK
    w_in_ext = jnp.concatenate([w_in, _rot_cols(w_in[:, rk:])], axis=1).astype(BF16)
    c = norm_matmul(x2, norm_g, w_in_ext)

    wq = w_q_up.reshape(Q_RANK, nh, QK_NOPE + QK_ROPE)
    rope_w = wq[..., QK_NOPE:]
    z = jnp.zeros_like(rope_w)
    wq3 = jnp.concatenate([wq[..., :QK_NOPE], rope_w, z, _rot_cols(rope_w), z], axis=-1)
    wq3 = wq3.reshape(Q_RANK, nh * 3 * LANES).astype(BF16)
    wkv = w_kv_up.astype(BF16)

    inv = 1.0 / (ROPE_THETA ** (jnp.arange(0, QK_ROPE, 2, dtype=F32) / QK_ROPE))
    ang = jnp.arange(L, dtype=F32)[:, None] * inv[None, :]
    ang = jnp.concatenate([ang, ang], axis=-1)
    zpad = jnp.zeros((L, LANES - QK_ROPE), F32)
    cos_k = jnp.concatenate([jnp.cos(ang), zpad], axis=1)
    sin_k = jnp.concatenate([jnp.sin(ang), zpad], axis=1)
    cos_q = cos_k * ATTN_SCALE
    sin_q = sin_k * ATTN_SCALE

    tr = _row_tile(L, 256)
    nl = L // tr
    cw = c.shape[1]
    tab = pl.BlockSpec((tr, LANES), lambda i: (i % nl, 0))
    slab = pl.BlockSpec((None, nh, tr, HEAD_W), lambda i: (i // nl, 0, i % nl, 0))
    slab_shape = jax.ShapeDtypeStruct((b, nh, L, HEAD_W), BF16)
    q, k, v = pl.pallas_call(
        functools.partial(_mla_qkv_kernel, n_heads=nh, q_rank=Q_RANK, kv_rank=KV_RANK),
        grid=(t // tr,),
        in_specs=[pl.BlockSpec((tr, cw), lambda i: (i, 0)),
                  pl.BlockSpec((1, Q_RANK), lambda i: (0, 0)),
                  pl.BlockSpec((1, KV_RANK), lambda i: (0, 0)),
                  pl.BlockSpec(wq3.shape, lambda i: (0, 0)),
                  pl.BlockSpec(wkv.shape, lambda i: (0, 0)),
                  tab, tab, tab, tab],
        out_specs=[slab, slab, slab],
        out_shape=[slab_shape, slab_shape, slab_shape],
        compiler_params=pltpu.CompilerParams(
            dimension_semantics=("parallel",), vmem_limit_bytes=VMEM_LIMIT),
        name="mla_qkv",
    )(c, q_norm.reshape(1, -1).astype(F32), kv_norm.reshape(1, -1).astype(F32), wq3, wkv,
      cos_q, sin_q, cos_k, sin_k)

    o = mla_attention(q, k, v, tq=tq, tk=tk)

    out = matmul_residual(o.reshape(t, nh * V_DIM), w_out.astype(BF16), x2)
    return out.reshape(b, L, d)


def _fft_split(L):
    n = 2 * L
    n2 = min(256, n // 16)
    return n // n2, n2


def _fft_tables(L):
    n1, n2 = _fft_split(L)
    n = n1 * n2
    n1h = n1 // 2
    k1n = n1 // 2 + 1
    k1p = min(n1, -(-k1n // 8) * 8)
    g = FFT_GROUP
    nb = n2 // g
    k1 = jnp.arange(k1p, dtype=jnp.int32)
    m1 = jnp.arange(n1h, dtype=jnp.int32)
    m2 = jnp.arange(n2, dtype=jnp.int32)
    live = (k1 < k1n).astype(F32)
    wgt = jnp.where((k1 == 0) | (k1 == n1 // 2), 1.0, 2.0) * live / n1
    th1 = ((k1[:, None] * m1[None, :]) % n1).astype(F32) * (2.0 * math.pi / n1)
    c1, s1 = jnp.cos(th1), jnp.sin(th1)
    eye = jnp.eye(g, dtype=F32)
    fa = jnp.concatenate([jnp.kron(c1 * live[:, None], eye), jnp.kron(-s1 * live[:, None], eye)],
                         axis=0).astype(BF16)
    gc = jnp.concatenate([jnp.kron((c1 * wgt[:, None]).T, eye), jnp.kron((-s1 * wgt[:, None]).T, eye)],
                         axis=1).astype(BF16)
    tht = ((k1[:, None] * m2[None, :]) % n).astype(F32) * (2.0 * math.pi / n)
    tw = jnp.stack([jnp.cos(tht), -jnp.sin(tht)], axis=0).reshape(2, k1p, nb, g)
    tw = tw.transpose(0, 2, 1, 3).reshape(2, nb, k1p * g, 1)
    idx2 = (m2[:, None] * m2[None, :]) % n2
    th2 = idx2.astype(F32) * (2.0 * math.pi / n2)
    c2, s2 = jnp.cos(th2), jnp.sin(th2)
    mf = jnp.concatenate([jnp.concatenate([c2, s2], 1), jnp.concatenate([-s2, c2], 1)], 0).astype(BF16)
    mi = (jnp.concatenate([jnp.concatenate([c2, -s2], 1), jnp.concatenate([s2, c2], 1)], 0) / n2).astype(BF16)
    return fa, gc, tw, mf, mi


def _fft_a_kernel(x_ref, fa_ref, tw_ref, y_ref):
    n1h, g, tc = x_ref.shape
    y = jnp.dot(fa_ref[...], x_ref[...].reshape(n1h * g, tc), preferred_element_type=F32)
    half = y.shape[0] // 2
    yr, yi = y[:half], y[half:]
    tr, ti = tw_ref[0], tw_ref[1]
    y_ref[0] = (yr * tr - yi * ti).reshape(y_ref.shape[1:]).astype(y_ref.dtype)
    y_ref[1] = (yr * ti + yi * tr).reshape(y_ref.shape[1:]).astype(y_ref.dtype)


def fft_stage_a(x, fa, tw, tc=512):
    s, L, d = x.shape
    rows, cols = fa.shape
    nb = tw.shape[1]
    g = FFT_GROUP
    n1h, k1p, n2 = cols // g, rows // (2 * g), nb * g
    tc = _row_tile(d, tc)
    return pl.pallas_call(
        _fft_a_kernel,
        grid=(s, nb, d // tc),
        in_specs=[pl.BlockSpec((None, n1h, g, tc), lambda b, j, c: (b, 0, j, c)),
                  pl.BlockSpec((rows, cols), lambda b, j, c: (0, 0)),
                  pl.BlockSpec((2, None, k1p * g, 1), lambda b, j, c: (0, j, 0, 0))],
        out_specs=pl.BlockSpec((None, 2, k1p, g, tc), lambda b, j, c: (b, 0, 0, j, c)),
        out_shape=jax.ShapeDtypeStruct((s, 2, k1p, n2, d), BF16),
        compiler_params=pltpu.CompilerParams(
            dimension_semantics=("parallel", "parallel", "parallel"), vmem_limit_bytes=VMEM_LIMIT),
        name="fft_stage_a",
    )(x.reshape(s, n1h, n2, d), fa, tw)


def _fft_b_kernel(y_ref, h_ref, mf_ref, mi_ref, u_ref):
    n2 = y_ref.shape[1]
    tc = y_ref.shape[2]
    z = jnp.dot(mf_ref[...], y_ref[...].reshape(2 * n2, tc), preferred_element_type=F32)
    zr, zi = z[:n2], z[n2:]
    hr, hi = h_ref[0], h_ref[1]
    w = jnp.concatenate([zr * hr - zi * hi, zr * hi + zi * hr], axis=0).astype(BF16)
    u = jnp.dot(mi_ref[...], w, preferred_element_type=F32)
    u_ref[0] = u[:n2].astype(u_ref.dtype)
    u_ref[1] = u[n2:].astype(u_ref.dtype)


def fft_stage_b(y, spec, order, mf, mi, tc=1024):
    s, _, k1p, n2, d = y.shape
    tc = _row_tile(d, tc)
    blk = pl.BlockSpec((None, 2, None, n2, tc), lambda k, j, b: (b, 0, k, 0, j))
    return pl.pallas_call(
        _fft_b_kernel,
        grid=(k1p, d // tc, s),
        in_specs=[blk,
                  pl.BlockSpec((None, 2, None, n2, tc), lambda k, j, b: (order, 0, k, 0, j)),
                  pl.BlockSpec(mf.shape, lambda k, j, b: (0, 0)),
                  pl.BlockSpec(mi.shape, lambda k, j, b: (0, 0))],
        out_specs=blk,
        out_shape=jax.ShapeDtypeStruct(y.shape, BF16),
        compiler_params=pltpu.CompilerParams(
            dimension_semantics=("parallel", "parallel", "arbitrary"), vmem_limit_bytes=VMEM_LIMIT),
        name="fft_stage_b",
    )(y, spec, mf, mi)


def _fft_c_kernel(u_ref, gc_ref, tw_ref, v_ref, g_ref, sk_ref, o_ref):
    _, k1p, g, tc = u_ref.shape
    rows = v_ref.shape[0] * g
    ur = u_ref[0].reshape(k1p * g, tc).astype(F32)
    ui = u_ref[1].reshape(k1p * g, tc).astype(F32)
    tr, ti = tw_ref[0], tw_ref[1]
    w = jnp.concatenate([ur * tr + ui * ti, ui * tr - ur * ti], axis=0).astype(BF16)
    y = jnp.dot(gc_ref[...], w, preferred_element_type=F32)
    v = v_ref[...].reshape(rows, tc).astype(F32)
    gate = g_ref[...].reshape(rows, tc).astype(F32)
    o_ref[...] = (gate * (y + v * sk_ref[...])).astype(o_ref.dtype).reshape(o_ref.shape)


def fft_stage_c(u, gc, tw, v, gate, skip, tc=512):
    s, _, k1p, n2, d = u.shape
    rows, cols = gc.shape
    g = FFT_GROUP
    nb = n2 // g
    n1h = rows // g
    L = n1h * n2
    tc = _row_tile(d, tc)
    row = pl.BlockSpec((None, n1h, g, tc), lambda b, j, c: (b, 0, j, c))
    out = pl.pallas_call(
        _fft_c_kernel,
        grid=(s, nb, d // tc),
        in_specs=[pl.BlockSpec((None, 2, k1p, g, tc), lambda b, j, c: (b, 0, 0, j, c)),
                  pl.BlockSpec((rows, cols), lambda b, j, c: (0, 0)),
                  pl.BlockSpec((2, None, k1p * g, 1), lambda b, j, c: (0, j, 0, 0)),
                  row, row,
                  pl.BlockSpec((1, tc), lambda b, j, c: (0, c))],
        out_specs=row,
        out_shape=jax.ShapeDtypeStruct((s, n1h, n2, d), BF16),
        compiler_params=pltpu.CompilerParams(
            dimension_semantics=("parallel", "parallel", "parallel"), vmem_limit_bytes=VMEM_LIMIT),
        name="fft_stage_c",
    )(u, gc, tw, v.reshape(s, n1h, n2, d), gate.reshape(s, n1h, n2, d), skip.reshape(1, d).astype(F32))
    return out.reshape(s, L, d)


def _filter_kernel(ft_ref, w1_ref, b1_ref, f1_ref, w2_ref, b2_ref, f2_ref, w3_ref, b3_ref, dl_ref,
                   taps_ref, sum_ref, *, seq_len):
    i = pl.program_id(1)
    tl = ft_ref.shape[0]
    a = jnp.sin(f1_ref[...] * (jnp.dot(ft_ref[...], w1_ref[...], preferred_element_type=F32) + b1_ref[...]))
    a = jnp.sin(f2_ref[...] * (jnp.dot(a, w2_ref[...], preferred_element_type=F32) + b2_ref[...]))
    f = jnp.dot(a, w3_ref[...], preferred_element_type=F32) + b3_ref[...]
    pos = (lax.broadcasted_iota(jnp.int32, (tl, 1), 0) + i * tl).astype(F32)
    t = pos * (1.0 / max(seq_len - 1, 1))
    f = f * jnp.exp(-t * dl_ref[...])

    @pl.when(i == 0)
    def _():
        sum_ref[...] = jnp.zeros_like(sum_ref)

    sum_ref[...] += jnp.sum(jnp.abs(f), axis=0, keepdims=True)
    is_bwd = pl.program_id(0) % 2 == 1
    f = jnp.where(jnp.logical_and(is_bwd, pos == 0.0), 0.0, f)
    taps_ref[...] = f.astype(taps_ref.dtype)


def hyena_filter_taps(L, d, f_w1, f_b1, f_fr1, f_w2, f_b2, f_fr2, f_w3, f_b3, tl=512):
    hp = LANES
    t = jnp.linspace(0.0, 1.0, L, dtype=F32)[:, None]
    w = (2.0 * math.pi / L) * jnp.arange(L, dtype=F32)[:, None]
    bands = jnp.linspace(1e-4, HY_BANDS - 1, HY_BANDS, dtype=F32)[None, :]
    feats = jnp.concatenate([t, jnp.cos(bands * w), -jnp.sin(bands * w),
                             jnp.zeros((L, hp - HY_EMB), F32)], axis=-1)

    def pad2(m, r, c):
        return jnp.pad(m.astype(F32), ((0, r - m.shape[0]), (0, c - m.shape[1])))

    w1 = pad2(f_w1, hp, hp)
    w2 = pad2(f_w2, hp, hp)
    w3 = pad2(f_w3, hp, f_w3.shape[1])
    b1, fr1 = pad2(f_b1[None], 1, hp), pad2(f_fr1[None], 1, hp)
    b2, fr2 = pad2(f_b2[None], 1, hp), pad2(f_fr2[None], 1, hp)
    max_decay = math.log(HY_TARGET) / HY_FAST_DECAY
    min_decay = math.log(HY_TARGET) / HY_SLOW_DECAY
    deltas = jnp.abs(jnp.linspace(min_decay, max_decay, d, dtype=F32))[None, :]
    ns = HY_ORDER * 2
    tl = _row_tile(L, tl)
    vec = lambda n: pl.BlockSpec((1, n), lambda s, i: (0, 0))
    mat = lambda r, c: pl.BlockSpec((r, c), lambda s, i: (0, 0))
    return pl.pallas_call(
        functools.partial(_filter_kernel, seq_len=L),
        grid=(ns, L // tl),
        in_specs=[pl.BlockSpec((tl, hp), lambda s, i: (i, 0)),
                  mat(hp, hp), vec(hp), vec(hp), mat(hp, hp), vec(hp), vec(hp),
                  pl.BlockSpec((hp, d), lambda s, i: (0, s)),
                  pl.BlockSpec((1, d), lambda s, i: (0, s)),
                  vec(d)],
        out_specs=[pl.BlockSpec((None, tl, d), lambda s, i: (s, i, 0)),
                   pl.BlockSpec((None, 1, d), lambda s, i: (s, 0, 0))],
        out_shape=[jax.ShapeDtypeStruct((ns, L, d), BF16), jax.ShapeDtypeStruct((ns, 1, d), F32)],
        compiler_params=pltpu.CompilerParams(dimension_semantics=("parallel", "arbitrary")),
        name="hyena_filter_taps",
    )(feats, w1, b1, fr1, w2, b2, fr2, w3, f_b3[None].astype(F32), deltas)


def _fft_bf_kernel(yf_ref, yb_ref, sf_ref, sb_ref, mf_ref, h_ref):
    n2 = yf_ref.shape[1]
    tc = yf_ref.shape[2]
    zf = jnp.dot(mf_ref[...], yf_ref[...].reshape(2 * n2, tc), preferred_element_type=F32)
    zb = jnp.dot(mf_ref[...], yb_ref[...].reshape(2 * n2, tc), preferred_element_type=F32)
    sf = 1.0 / (sf_ref[...] + 1e-6)
    sb = 1.0 / (sb_ref[...] + 1e-6)
    h_ref[0] = zf[:n2] * sf + zb[:n2] * sb
    h_ref[1] = zf[n2:] * sf - zb[n2:] * sb


def hyena_spectrum(L, d, fa, tw, mf, *filter_params, tc=1024):
    taps, sums = hyena_filter_taps(L, d, *filter_params)
    yv = fft_stage_a(taps, fa, tw)
    _, _, n1, n2, _ = yv.shape
    tc = _row_tile(d, tc)
    side = lambda sd: pl.BlockSpec((None, 2, None, n2, tc), lambda o, k, j: (2 * o + sd, 0, k, 0, j))
    norm = lambda sd: pl.BlockSpec((None, 1, tc), lambda o, k, j: (2 * o + sd, 0, j))
    return pl.pallas_call(
        _fft_bf_kernel,
        grid=(HY_ORDER, n1, d // tc),
        in_specs=[side(0), side(1), norm(0), norm(1), pl.BlockSpec(mf.shape, lambda o, k, j: (0, 0))],
        out_specs=pl.BlockSpec((None, 2, None, n2, tc), lambda o, k, j: (o, 0, k, 0, j)),
        out_shape=jax.ShapeDtypeStruct((HY_ORDER, 2, n1, n2, d), F32),
        compiler_params=pltpu.CompilerParams(
            dimension_semantics=("parallel", "parallel", "parallel"), vmem_limit_bytes=VMEM_LIMIT),
        name="fft_stage_b_filter",
    )(yv, yv, sums, sums, mf)


def _shortconv_kernel(u_ref, up_ref, un_ref, w_ref, b_ref, v_ref, g1_ref, g2_ref):
    i = pl.program_id(1)
    tr = u_ref.shape[0]
    d = v_ref.shape[1]
    hr = up_ref.shape[0]
    first = i == 0
    last = i == pl.num_programs(1) - 1
    for j, o_ref in enumerate((v_ref, g1_ref, g2_ref)):
        cols = slice(j * d, (j + 1) * d)
        u = u_ref[:, cols].astype(F32)
        row = lax.broadcasted_iota(jnp.int32, u.shape, 0)
        prev_row = jnp.where(first, 0.0, up_ref[hr - 1:hr, cols].astype(F32))
        next_row = jnp.where(last, 0.0, un_ref[0:1, cols].astype(F32))
        um = jnp.where(row == 0, prev_row, pltpu.roll(u, 1, axis=0))
        up = jnp.where(row == tr - 1, next_row, pltpu.roll(u, tr - 1, axis=0))
        w = w_ref[:, cols]
        o_ref[...] = (um * w[0:1] + u * w[1:2] + up * w[2:3] + b_ref[:, cols]).astype(o_ref.dtype)


def hyena_shortconv(u, conv_w, conv_b, tr=256):
    b, L, d3 = u.shape
    d = d3 // 3
    tr = _row_tile(L, tr)
    hr = 16
    hb = tr // hr
    nhb = L // hr
    out = pl.BlockSpec((None, tr, d), lambda bi, i: (bi, i, 0))
    shape = jax.ShapeDtypeStruct((b, L, d), BF16)
    return pl.pallas_call(
        _shortconv_kernel,
        grid=(b, L // tr),
        in_specs=[pl.BlockSpec((None, tr, d3), lambda bi, i: (bi, i, 0)),
                  pl.BlockSpec((None, hr, d3), lambda bi, i: (bi, jnp.maximum(i * hb - 1, 0), 0)),
                  pl.BlockSpec((None, hr, d3), lambda bi, i: (bi, jnp.minimum((i + 1) * hb, nhb - 1), 0)),
                  pl.BlockSpec((3, d3), lambda bi, i: (0, 0)),
                  pl.BlockSpec((1, d3), lambda bi, i: (0, 0))],
        out_specs=[out, out, out],
        out_shape=[shape, shape, shape],
        compiler_params=pltpu.CompilerParams(dimension_semantics=("parallel", "parallel")),
        name="hyena_shortconv",
    )(u, u, u, conv_w.astype(F32), conv_b[None].astype(F32))


def hyena_block(x, norm_g, w_in, conv_w, conv_b, f_w1, f_b1, f_fr1, f_w2, f_b2, f_fr2, f_w3, f_b3,
                skip, w_out):
    b, L, d = x.shape
    t = b * L
    x2 = x.reshape(t, d)
    u = norm_matmul(x2, norm_g, w_in.astype(BF16), out_dtype=BF16).reshape(b, L, 3 * d)
    v, g1, g2 = hyena_shortconv(u, conv_w, conv_b)
    fa, gc, tw, mf, mi = _fft_tables(L)
    spec = hyena_spectrum(L, d, fa, tw, mf, f_w1, f_b1, f_fr1, f_w2, f_b2, f_fr2, f_w3, f_b3)
    z = fft_stage_c(fft_stage_b(fft_stage_a(v, fa, tw), spec, 0, mf, mi), gc, tw, v, g1, skip[0])
    y = fft_stage_c(fft_stage_b(fft_stage_a(z, fa, tw), spec, 1, mf, mi), gc, tw, z, g2, skip[1])
    out = matmul_residual(y.reshape(t, d), w_out.astype(BF16), x2)
    return out.reshape(b, L, d)


def _router_kernel(x_ref, g_ref, w_ref, lg_ref, h_ref):
    h = _rms(x_ref[...].astype(F32), g_ref[...]).astype(BF16)
    h_ref[...] = h
    lg_ref[...] = jnp.dot(h, w_ref[...], preferred_element_type=F32)


def _expert_kernel(be_ref, nu_ref, x_ref, wg_ref, wu_ref, wd_ref, rw_ref, o_ref, wg_s, wu_s, wd_s):
    i = pl.program_id(0)
    used = i < nu_ref[0]

    @pl.when(jnp.logical_or(i == 0, be_ref[i] != be_ref[jnp.maximum(i - 1, 0)]))
    def _():
        wg_s[...] = wg_ref[...].astype(BF16)
        wu_s[...] = wu_ref[...].astype(BF16)
        wd_s[...] = wd_ref[...].astype(BF16)

    @pl.when(used)
    def _():
        x = x_ref[...]
        g = jnp.dot(x, wg_s[...], preferred_element_type=F32)
        u = jnp.dot(x, wu_s[...], preferred_element_type=F32)
        a = (g * jax.nn.sigmoid(g) * u).astype(BF16)
        y = jnp.dot(a, wd_s[...], preferred_element_type=F32)
        o_ref[...] = (y * rw_ref[...]).astype(o_ref.dtype)

    @pl.when(jnp.logical_not(used))
    def _():
        o_ref[...] = jnp.zeros_like(o_ref)


def moe_block(x, norm_g, w_rg, b_rg, w_re, b_re, w_gate, w_up, w_down, tm=256):
    nb, L, d = x.shape
    T = nb * L
    ne = N_EXPERTS
    x2 = x.reshape(T, d)
    nr = N_GROUPS + ne
    w_r = jnp.concatenate([w_rg, w_re, jnp.zeros((d, LANES - nr), w_rg.dtype)], axis=1).astype(BF16)
    tr = _row_tile(T, 512)
    logits, h = pl.pallas_call(
        _router_kernel,
        grid=(T // tr,),
        in_specs=[pl.BlockSpec((tr, d), lambda i: (i, 0)),
                  pl.BlockSpec((1, d), lambda i: (0, 0)),
                  pl.BlockSpec((d, LANES), lambda i: (0, 0))],
        out_specs=[pl.BlockSpec((tr, LANES), lambda i: (i, 0)),
                   pl.BlockSpec((tr, d), lambda i: (i, 0))],
        out_shape=[jax.ShapeDtypeStruct((T, LANES), F32), jax.ShapeDtypeStruct((T, d), BF16)],
        compiler_params=pltpu.CompilerParams(dimension_semantics=("parallel",)),
        name="moe_router",
    )(x2, norm_g.reshape(1, d).astype(F32), w_r)

    lg = logits[:, :N_GROUPS] + b_rg.astype(F32)
    pg = jax.nn.softmax(lg, axis=-1)
    gsel = jnp.argmax(lg, axis=-1).astype(jnp.int32)
    pg_sel = jnp.take_along_axis(pg, gsel[:, None], axis=1)
    le = (logits[:, N_GROUPS:nr] + b_re.astype(F32)).reshape(T, N_GROUPS, EXPERTS_PER_GROUP)
    le = jnp.take_along_axis(le, gsel[:, None, None], axis=1)[:, 0]
    pe = jax.nn.softmax(le, axis=-1)
    top_p, top_i = lax.top_k(pe, TOP_K)
    gate = pg_sel * top_p / jnp.sum(top_p, axis=-1, keepdims=True)
    eid = (gsel[:, None] * EXPERTS_PER_GROUP + top_i.astype(jnp.int32)).reshape(-1)
    gw = gate.reshape(-1)

    na = T * TOP_K
    order = jnp.argsort(eid).astype(jnp.int32)
    inv = jnp.argsort(order).astype(jnp.int32)
    experts = jnp.arange(ne, dtype=jnp.int32)
    counts = jnp.sum((eid[:, None] == experts[None, :]).astype(jnp.int32), axis=0)
    start = jnp.cumsum(counts) - counts
    pcounts = (counts + tm - 1) // tm * tm
    pend = jnp.cumsum(pcounts)
    pstart = pend - pcounts
    n_rows = na + ne * tm
    n_blk = n_rows // tm
    pos = pstart[eid] + inv - start[eid]
    blk_first = jnp.arange(n_blk, dtype=jnp.int32) * tm
    blk_e = jnp.minimum(jnp.sum((pend[None, :] <= blk_first[:, None]).astype(jnp.int32), axis=1), ne - 1)
    row_e = jnp.repeat(blk_e, tm)
    off = jnp.arange(n_rows, dtype=jnp.int32) - pstart[row_e]
    live = off < counts[row_e]
    row_a = order[jnp.clip(start[row_e] + off, 0, na - 1)]
    row_tok = jnp.where(live, row_a // TOP_K, 0)
    row_w = jnp.where(live, gw[row_a], 0.0)
    n_used = (pend[-1:] // tm).astype(jnp.int32)

    xs = h[row_tok]
    de = w_gate.shape[-1]
    y = pl.pallas_call(
        _expert_kernel,
        grid_spec=pltpu.PrefetchScalarGridSpec(
            num_scalar_prefetch=2,
            grid=(n_blk,),
            in_specs=[pl.BlockSpec((tm, d), lambda i, be, nu: (i, 0)),
                      pl.BlockSpec((None, d, de), lambda i, be, nu: (be[i], 0, 0)),
                      pl.BlockSpec((None, d, de), lambda i, be, nu: (be[i], 0, 0)),
                      pl.BlockSpec((None, de, d), lambda i, be, nu: (be[i], 0, 0)),
                      pl.BlockSpec((tm, 1), lambda i, be, nu: (i, 0))],
            out_specs=pl.BlockSpec((tm, d), lambda i, be, nu: (i, 0)),
            scratch_shapes=[pltpu.VMEM((d, de), BF16), pltpu.VMEM((d, de), BF16),
                            pltpu.VMEM((de, d), BF16)]),
        out_shape=jax.ShapeDtypeStruct((n_rows, d), BF16),
        compiler_params=pltpu.CompilerParams(
            dimension_semantics=("arbitrary",), vmem_limit_bytes=VMEM_LIMIT),
        name="moe_experts",
    )(blk_e, n_used, xs, w_gate, w_up, w_down, row_w[:, None])

    pos = pos.reshape(T, TOP_K)
    out = x2 + y[pos[:, 0]].astype(F32) + y[pos[:, 1]].astype(F32)
    return out.reshape(nb, L, d)


def encoder_trunk(x, norm_mix, norm_ffn, norm_final, mla_p, hy_p, moe_p):
    for i in range(DEPTH):
        j = i // 2
        if i % 2 == 0:
            x = mla_block(x, norm_mix[i], *[p[j] for p in mla_p])
        else:
            x = hyena_block(x, norm_mix[i], *[p[j] for p in hy_p])
        x = moe_block(x, norm_ffn[i], *[p[i] for p in moe_p])
    return rms_norm_pallas(x, norm_final)


def kernel(x_prompt, x_sample, norm_mix, norm_ffn, norm_final, mla_w_in, mla_q_norm, mla_w_q_up, mla_kv_norm, mla_w_kv_up, mla_w_out, hy_w_in, hy_conv_w, hy_conv_b, hy_f_w1, hy_f_b1, hy_f_freq1, hy_f_w2, hy_f_b2, hy_f_freq2, hy_f_w3, hy_f_b3, hy_skip, hy_w_out, moe_w_group, moe_b_group, moe_w_expert, moe_b_expert, moe_w_gate, moe_w_up, moe_w_down):
    mla_p = (mla_w_in, mla_q_norm, mla_w_q_up, mla_kv_norm, mla_w_kv_up, mla_w_out)
    hy_p = (hy_w_in, hy_conv_w, hy_conv_b, hy_f_w1, hy_f_b1, hy_f_freq1, hy_f_w2, hy_f_b2,
            hy_f_freq2, hy_f_w3, hy_f_b3, hy_skip, hy_w_out)
    moe_p = (moe_w_group, moe_b_group, moe_w_expert, moe_b_expert, moe_w_gate, moe_w_up, moe_w_down)
    y_prompt = encoder_trunk(x_prompt, norm_mix, norm_ffn, norm_final, mla_p, hy_p, moe_p)
    y_sample = encoder_trunk(x_sample, norm_mix, norm_ffn, norm_final, mla_p, hy_p, moe_p)
    return (y_prompt, y_sample)
```

```python
import functools
import math
import jax
import jax.numpy as jnp
from jax import lax
from jax.experimental import pallas as pl
from jax.experimental.pallas import tpu as pltpu

D_MODEL = 2048
DEPTH = 2
N_HEADS = 16
Q_RANK = 512
KV_RANK = 512
QK_NOPE = 128
QK_ROPE = 64
V_DIM = 128
ROPE_THETA = 10000.0
ATTN_SCALE = 1.0 / math.sqrt(QK_NOPE + QK_ROPE)
HY_ORDER = 2
HY_EMB = 33
HY_BANDS = (HY_EMB - 1) // 2
HY_HIDDEN = 64
HY_FAST_DECAY = 0.3
HY_SLOW_DECAY = 1.5
HY_TARGET = 1e-2
N_GROUPS = 8
EXPERTS_PER_GROUP = 8
N_EXPERTS = N_GROUPS * EXPERTS_PER_GROUP
TOP_K = 2
D_EXPERT = 512
ROW_BLOCK = 128
NORM_EPS = 1e-6

F32 = jnp.float32
BF16 = jnp.bfloat16
LANES = 128
HEAD_W = 2 * LANES
VMEM_LIMIT = 56 * 1024 * 1024
FFT_GROUP = 16


def _row_tile(n, want):
    t = min(n, want)
    while n % t:
        t //= 2
    return t


def _rms(x, g):
    return x * lax.rsqrt(jnp.mean(x * x, axis=-1, keepdims=True) + NORM_EPS) * g


def _norm_mm_kernel(x_ref, g_ref, w_ref, o_ref, h_scr):
    @pl.when(pl.program_id(1) == 0)
    def _():
        h_scr[...] = _rms(x_ref[...].astype(F32), g_ref[...]).astype(BF16)

    o_ref[...] = jnp.dot(h_scr[...], w_ref[...], preferred_element_type=F32).astype(o_ref.dtype)


def norm_matmul(x, g, w, out_dtype=F32, tm=512, tn=1024):
    m, k = x.shape
    n = w.shape[1]
    tm = _row_tile(m, tm)
    tn = n if n <= 2048 else _row_tile(n, tn)
    return pl.pallas_call(
        _norm_mm_kernel,
        grid=(m // tm, n // tn),
        in_specs=[pl.BlockSpec((tm, k), lambda i, j: (i, 0)),
                  pl.BlockSpec((1, k), lambda i, j: (0, 0)),
                  pl.BlockSpec((k, tn), lambda i, j: (0, j))],
        out_specs=pl.BlockSpec((tm, tn), lambda i, j: (i, j)),
        out_shape=jax.ShapeDtypeStruct((m, n), out_dtype),
        scratch_shapes=[pltpu.VMEM((tm, k), BF16)],
        compiler_params=pltpu.CompilerParams(
            dimension_semantics=("parallel", "arbitrary"), vmem_limit_bytes=VMEM_LIMIT),
        name="norm_matmul",
    )(x, g.reshape(1, k).astype(F32), w)


def _mm_res_kernel(a_ref, w_ref, r_ref, o_ref):
    acc = jnp.dot(a_ref[...].astype(BF16), w_ref[...], preferred_element_type=F32)
    o_ref[...] = (r_ref[...].astype(F32) + acc).astype(o_ref.dtype)


def matmul_residual(a, w, r, tm=512, tn=1024):
    m, k = a.shape
    n = w.shape[1]
    tm = _row_tile(m, tm)
    tn = _row_tile(n, tn)
    return pl.pallas_call(
        _mm_res_kernel,
        grid=(m // tm, n // tn),
        in_specs=[pl.BlockSpec((tm, k), lambda i, j: (i, 0)),
                  pl.BlockSpec((k, tn), lambda i, j: (0, j)),
                  pl.BlockSpec((tm, tn), lambda i, j: (i, j))],
        out_specs=pl.BlockSpec((tm, tn), lambda i, j: (i, j)),
        out_shape=jax.ShapeDtypeStruct((m, n), r.dtype),
        compiler_params=pltpu.CompilerParams(
            dimension_semantics=("parallel", "parallel"), vmem_limit_bytes=VMEM_LIMIT),
        name="matmul_residual",
    )(a, w, r)


def _rms_kernel(x_ref, g_ref, o_ref):
    o_ref[...] = _rms(x_ref[...].astype(F32), g_ref[...]).astype(o_ref.dtype)


def rms_norm_pallas(x, g, rows=512):
    shp = x.shape
    d = shp[-1]
    x2 = x.reshape(-1, d)
    t = x2.shape[0]
    rows = _row_tile(t, rows)
    out = pl.pallas_call(
        _rms_kernel,
        grid=(t // rows,),
        in_specs=[pl.BlockSpec((rows, d), lambda i: (i, 0)),
                  pl.BlockSpec((1, d), lambda i: (0, 0))],
        out_specs=pl.BlockSpec((rows, d), lambda i: (i, 0)),
        out_shape=jax.ShapeDtypeStruct((t, d), x.dtype),
        compiler_params=pltpu.CompilerParams(dimension_semantics=("parallel",)),
        name="rms_norm",
    )(x2, g.reshape(1, d).astype(F32))
    return out.reshape(shp)


def _mla_qkv_kernel(c_ref, qg_ref, kvg_ref, wq_ref, wkv_ref, cq_ref, sq_ref, ck_ref, sk_ref,
                    q_ref, k_ref, v_ref, *, n_heads, q_rank, kv_rank):
    c = c_ref[...]
    cqn = _rms(c[:, :q_rank], qg_ref[...]).astype(BF16)
    ckvn = _rms(c[:, q_rank:q_rank + kv_rank], kvg_ref[...]).astype(BF16)
    kr = c[:, q_rank + kv_rank:]
    k_hi = (kr * ck_ref[...] + pltpu.roll(kr, LANES // 2, axis=1) * sk_ref[...]).astype(BF16)
    tr = c.shape[0]
    lane = lax.broadcasted_iota(jnp.int32, (tr, LANES), 1)
    v_hi = jnp.where(lane == 0, 1.0, 0.0).astype(BF16)
    cq = cq_ref[...]
    sq = sq_ref[...]
    qw = 3 * LANES
    for h in range(n_heads):
        a = jnp.dot(cqn, wq_ref[:, h * qw:(h + 1) * qw], preferred_element_type=F32)
        q_ref[h, :, :LANES] = (a[:, :LANES] * ATTN_SCALE).astype(BF16)
        q_ref[h, :, LANES:] = (a[:, LANES:2 * LANES] * cq + a[:, 2 * LANES:] * sq).astype(BF16)
        kv = jnp.dot(ckvn, wkv_ref[:, h * HEAD_W:(h + 1) * HEAD_W], preferred_element_type=F32)
        k_ref[h, :, :LANES] = kv[:, :LANES].astype(BF16)
        k_ref[h, :, LANES:] = k_hi
        v_ref[h, :, :LANES] = kv[:, LANES:].astype(BF16)
        v_ref[h, :, LANES:] = v_hi


def _attn_kernel(q_ref, k_ref, v_ref, o_ref, *, tk, unroll):
    q = q_ref[...]
    tq = q.shape[0]
    nk = k_ref.shape[0] // tk

    def body(j, carry):
        m, acc = carry
        off = pl.multiple_of(j * tk, tk)
        s = lax.dot_general(q, k_ref[pl.ds(off, tk), :], (((1,), (1,)), ((), ())),
                            preferred_element_type=F32)
        m_new = jnp.maximum(m, jnp.max(s, axis=1, keepdims=True))
        p = jnp.exp((s - m_new).astype(BF16))
        acc = acc * jnp.exp(m - m_new) + jnp.dot(p, v_ref[pl.ds(off, tk), :],
                                                 preferred_element_type=F32)
        return m_new, acc

    m0 = jnp.full((tq, 1), -1e30, F32)
    acc0 = jnp.zeros((tq, HEAD_W), F32)
    _, acc = lax.fori_loop(0, nk, body, (m0, acc0), unroll=min(unroll, nk))
    o_ref[...] = (acc[:, :LANES] / acc[:, LANES:LANES + 1]).astype(o_ref.dtype)


def mla_attention(q, k, v, tq=512, tk=1024, unroll=16):
    b, nh, L, _ = q.shape
    tq = _row_tile(L, tq)
    tk = _row_tile(L, tk)
    kv_spec = pl.BlockSpec((None, None, L, HEAD_W), lambda bi, h, i: (bi, h, 0, 0))
    return pl.pallas_call(
        functools.partial(_attn_kernel, tk=tk, unroll=unroll),
        grid=(b, nh, L // tq),
        in_specs=[pl.BlockSpec((None, None, tq, HEAD_W), lambda bi, h, i: (bi, h, i, 0)),
                  kv_spec, kv_spec],
        out_specs=pl.BlockSpec((None, tq, V_DIM), lambda bi, h, i: (bi, i, h)),
        out_shape=jax.ShapeDtypeStruct((b, L, nh * V_DIM), BF16),
        compiler_params=pltpu.CompilerParams(
            dimension_semantics=("parallel", "parallel", "arbitrary"),
            vmem_limit_bytes=VMEM_LIMIT),
        name="mla_attention",
    )(q, k, v)


def _rot_cols(w):
    half = w.shape[-1] // 2
    return jnp.concatenate([-w[..., half:], w[..., :half]], axis=-1)


def mla_block(x, norm_g, w_in, q_norm, w_q_up, kv_norm, w_kv_up, w_out, tq=512, tk=1024):
    assert QK_NOPE == LANES and V_DIM == LANES and QK_ROPE == LANES // 2
    b, L, d = x.shape
    t = b * L
    nh = N_HEADS
    x2 = x.reshape(t, d)
    rk = Q_RANK + KV_RANK
    w_in_ext = jnp.concatenate([w_in, _rot_cols(w_in[:, rk:])], axis=1).astype(BF16)
    c = norm_matmul(x2, norm_g, w_in_ext)

    wq = w_q_up.reshape(Q_RANK, nh, QK_NOPE + QK_ROPE)
    rope_w = wq[..., QK_NOPE:]
    z = jnp.zeros_like(rope_w)
    wq3 = jnp.concatenate([wq[..., :QK_NOPE], rope_w, z, _rot_cols(rope_w), z], axis=-1)
    wq3 = wq3.reshape(Q_RANK, nh * 3 * LANES).astype(BF16)
    wkv = w_kv_up.astype(BF16)

    inv = 1.0 / (ROPE_THETA ** (jnp.arange(0, QK_ROPE, 2, dtype=F32) / QK_ROPE))
    ang = jnp.arange(L, dtype=F32)[:, None] * inv[None, :]
    ang = jnp.concatenate([ang, ang], axis=-1)
    zpad = jnp.zeros((L, LANES - QK_ROPE), F32)
    cos_k = jnp.concatenate([jnp.cos(ang), zpad], axis=1)
    sin_k = jnp.concatenate([jnp.sin(ang), zpad], axis=1)
    cos_q = cos_k * ATTN_SCALE
    sin_q = sin_k * ATTN_SCALE

    tr = _row_tile(L, 256)
    nl = L // tr
    cw = c.shape[1]
    tab = pl.BlockSpec((tr, LANES), lambda i: (i % nl, 0))
    slab = pl.BlockSpec((None, nh, tr, HEAD_W), lambda i: (i // nl, 0, i % nl, 0))
    slab_shape = jax.ShapeDtypeStruct((b, nh, L, HEAD_W), BF16)
    q, k, v = pl.pallas_call(
        functools.partial(_mla_qkv_kernel, n_heads=nh, q_rank=Q_RANK, kv_rank=KV_RANK),
        grid=(t // tr,),
        in_specs=[pl.BlockSpec((tr, cw), lambda i: (i, 0)),
                  pl.BlockSpec((1, Q_RANK), lambda i: (0, 0)),
                  pl.BlockSpec((1, KV_RANK), lambda i: (0, 0)),
                  pl.BlockSpec(wq3.shape, lambda i: (0, 0)),
                  pl.BlockSpec(wkv.shape, lambda i: (0, 0)),
                  tab, tab, tab, tab],
        out_specs=[slab, slab, slab],
        out_shape=[slab_shape, slab_shape, slab_shape],
        compiler_params=pltpu.CompilerParams(
            dimension_semantics=("parallel",), vmem_limit_bytes=VMEM_LIMIT),
        name="mla_qkv",
    )(c, q_norm.reshape(1, -1).astype(F32), kv_norm.reshape(1, -1).astype(F32), wq3, wkv,
      cos_q, sin_q, cos_k, sin_k)

    o = mla_attention(q, k, v, tq=tq, tk=tk)

    out = matmul_residual(o.reshape(t, nh * V_DIM), w_out.astype(BF16), x2)
    return out.reshape(b, L, d)


def _fft_split(L):
    n = 2 * L
    n2 = min(256, n // 16)
    return n // n2, n2


def _fft_tables(L):
    n1, n2 = _fft_split(L)
    n = n1 * n2
    n1h = n1 // 2
    k1n = n1 // 2 + 1
    k1p = min(n1, -(-k1n // 8) * 8)
    g = FFT_GROUP
    nb = n2 // g
    k1 = jnp.arange(k1p, dtype=jnp.int32)
    m1 = jnp.arange(n1h, dtype=jnp.int32)
    m2 = jnp.arange(n2, dtype=jnp.int32)
    live = (k1 < k1n).astype(F32)
    wgt = jnp.where((k1 == 0) | (k1 == n1 // 2), 1.0, 2.0) * live / n1
    th1 = ((k1[:, None] * m1[None, :]) % n1).astype(F32) * (2.0 * math.pi / n1)
    c1, s1 = jnp.cos(th1), jnp.sin(th1)
    eye = jnp.eye(g, dtype=F32)
    fa = jnp.concatenate([jnp.kron(c1 * live[:, None], eye), jnp.kron(-s1 * live[:, None], eye)],
                         axis=0).astype(BF16)
    gc = jnp.concatenate([jnp.kron((c1 * wgt[:, None]).T, eye), jnp.kron((-s1 * wgt[:, None]).T, eye)],
                         axis=1).astype(BF16)
    tht = ((k1[:, None] * m2[None, :]) % n).astype(F32) * (2.0 * math.pi / n)
    tw = jnp.stack([jnp.cos(tht), -jnp.sin(tht)], axis=0).reshape(2, k1p, nb, g)
    tw = tw.transpose(0, 2, 1, 3).reshape(2, nb, k1p * g, 1)
    idx2 = (m2[:, None] * m2[None, :]) % n2
    th2 = idx2.astype(F32) * (2.0 * math.pi / n2)
    c2, s2 = jnp.cos(th2), jnp.sin(th2)
    mf = jnp.concatenate([jnp.concatenate([c2, s2], 1), jnp.concatenate([-s2, c2], 1)], 0).astype(BF16)
    mi = (jnp.concatenate([jnp.concatenate([c2, -s2], 1), jnp.concatenate([s2, c2], 1)], 0) / n2).astype(BF16)
    return fa, gc, tw, mf, mi


def _fft_a_kernel(x_ref, fa_ref, tw_ref, y_ref):
    n1h, g, tc = x_ref.shape
    y = jnp.dot(fa_ref[...], x_ref[...].reshape(n1h * g, tc), preferred_element_type=F32)
    half = y.shape[0] // 2
    yr, yi = y[:half], y[half:]
    tr, ti = tw_ref[0], tw_ref[1]
    y_ref[0] = (yr * tr - yi * ti).reshape(y_ref.shape[1:]).astype(y_ref.dtype)
    y_ref[1] = (yr * ti + yi * tr).reshape(y_ref.shape[1:]).astype(y_ref.dtype)


def fft_stage_a(x, fa, tw, tc=512):
    s, L, d = x.shape
    rows, cols = fa.shape
    nb = tw.shape[1]
    g = FFT_GROUP
    n1h, k1p, n2 = cols // g, rows // (2 * g), nb * g
    tc = _row_tile(d, tc)
    return pl.pallas_call(
        _fft_a_kernel,
        grid=(s, nb, d // tc),
        in_specs=[pl.BlockSpec((None, n1h, g, tc), lambda b, j, c: (b, 0, j, c)),
                  pl.BlockSpec((rows, cols), lambda b, j, c: (0, 0)),
                  pl.BlockSpec((2, None, k1p * g, 1), lambda b, j, c: (0, j, 0, 0))],
        out_specs=pl.BlockSpec((None, 2, k1p, g, tc), lambda b, j, c: (b, 0, 0, j, c)),
        out_shape=jax.ShapeDtypeStruct((s, 2, k1p, n2, d), BF16),
        compiler_params=pltpu.CompilerParams(
            dimension_semantics=("parallel", "parallel", "parallel"), vmem_limit_bytes=VMEM_LIMIT),
        name="fft_stage_a",
    )(x.reshape(s, n1h, n2, d), fa, tw)


def _fft_b_kernel(y_ref, h_ref, mf_ref, mi_ref, u_ref):
    n2 = y_ref.shape[1]
    tc = y_ref.shape[2]
    z = jnp.dot(mf_ref[...], y_ref[...].reshape(2 * n2, tc), preferred_element_type=F32)
    zr, zi = z[:n2], z[n2:]
    hr, hi = h_ref[0], h_ref[1]
    w = jnp.concatenate([zr * hr - zi * hi, zr * hi + zi * hr], axis=0).astype(BF16)
    u = jnp.dot(mi_ref[...], w, preferred_element_type=F32)
    u_ref[0] = u[:n2].astype(u_ref.dtype)
    u_ref[1] = u[n2:].astype(u_ref.dtype)


def fft_stage_b(y, spec, order, mf, mi, tc=1024):
    s, _, k1p, n2, d = y.shape
    tc = _row_tile(d, tc)
    blk = pl.BlockSpec((None, 2, None, n2, tc), lambda k, j, b: (b, 0, k, 0, j))
    return pl.pallas_call(
        _fft_b_kernel,
        grid=(k1p, d // tc, s),
        in_specs=[blk,
                  pl.BlockSpec((None, 2, None, n2, tc), lambda k, j, b: (order, 0, k, 0, j)),
                  pl.BlockSpec(mf.shape, lambda k, j, b: (0, 0)),
                  pl.BlockSpec(mi.shape, lambda k, j, b: (0, 0))],
        out_specs=blk,
        out_shape=jax.ShapeDtypeStruct(y.shape, BF16),
        compiler_params=pltpu.CompilerParams(
            dimension_semantics=("parallel", "parallel", "arbitrary"), vmem_limit_bytes=VMEM_LIMIT),
        name="fft_stage_b",
    )(y, spec, mf, mi)


def _fft_c_kernel(u_ref, gc_ref, tw_ref, v_ref, g_ref, sk_ref, o_ref):
    _, k1p, g, tc = u_ref.shape
    rows = v_ref.shape[0] * g
    ur = u_ref[0].reshape(k1p * g, tc).astype(F32)
    ui = u_ref[1].reshape(k1p * g, tc).astype(F32)
    tr, ti = tw_ref[0], tw_ref[1]
    w = jnp.concatenate([ur * tr + ui * ti, ui * tr - ur * ti], axis=0).astype(BF16)
    y = jnp.dot(gc_ref[...], w, preferred_element_type=F32)
    v = v_ref[...].reshape(rows, tc).astype(F32)
    gate = g_ref[...].reshape(rows, tc).astype(F32)
    o_ref[...] = (gate * (y + v * sk_ref[...])).astype(o_ref.dtype).reshape(o_ref.shape)


def fft_stage_c(u, gc, tw, v, gate, skip, tc=512):
    s, _, k1p, n2, d = u.shape
    rows, cols = gc.shape
    g = FFT_GROUP
    nb = n2 // g
    n1h = rows // g
    L = n1h * n2
    tc = _row_tile(d, tc)
    row = pl.BlockSpec((None, n1h, g, tc), lambda b, j, c: (b, 0, j, c))
    out = pl.pallas_call(
        _fft_c_kernel,
        grid=(s, nb, d // tc),
        in_specs=[pl.BlockSpec((None, 2, k1p, g, tc), lambda b, j, c: (b, 0, 0, j, c)),
                  pl.BlockSpec((rows, cols), lambda b, j, c: (0, 0)),
                  pl.BlockSpec((2, None, k1p * g, 1), lambda b, j, c: (0, j, 0, 0)),
                  row, row,
                  pl.BlockSpec((1, tc), lambda b, j, c: (0, c))],
        out_specs=row,
        out_shape=jax.ShapeDtypeStruct((s, n1h, n2, d), BF16),
        compiler_params=pltpu.CompilerParams(
            dimension_semantics=("parallel", "parallel", "parallel"), vmem_limit_bytes=VMEM_LIMIT),
        name="fft_stage_c",
    )(u, gc, tw, v.reshape(s, n1h, n2, d), gate.reshape(s, n1h, n2, d), skip.reshape(1, d).astype(F32))
    return out.reshape(s, L, d)


def _filter_kernel(ft_ref, w1_ref, b1_ref, f1_ref, w2_ref, b2_ref, f2_ref, w3_ref, b3_ref, dl_ref,
                   taps_ref, sum_ref, *, seq_len):
    i = pl.program_id(1)
    tl = ft_ref.shape[0]
    a = jnp.sin(f1_ref[...] * (jnp.dot(ft_ref[...], w1_ref[...], preferred_element_type=F32) + b1_ref[...]))
    a = jnp.sin(f2_ref[...] * (jnp.dot(a, w2_ref[...], preferred_element_type=F32) + b2_ref[...]))
    f = jnp.dot(a, w3_ref[...], preferred_element_type=F32) + b3_ref[...]
    pos = (lax.broadcasted_iota(jnp.int32, (tl, 1), 0) + i * tl).astype(F32)
    t = pos * (1.0 / max(seq_len - 1, 1))
    f = f * jnp.exp(-t * dl_ref[...])

    @pl.when(i == 0)
    def _():
        sum_ref[...] = jnp.zeros_like(sum_ref)

    sum_ref[...] += jnp.sum(jnp.abs(f), axis=0, keepdims=True)
    is_bwd = pl.program_id(0) % 2 == 1
    f = jnp.where(jnp.logical_and(is_bwd, pos == 0.0), 0.0, f)
    taps_ref[...] = f.astype(taps_ref.dtype)


def hyena_filter_taps(L, d, f_w1, f_b1, f_fr1, f_w2, f_b2, f_fr2, f_w3, f_b3, tl=512):
    hp = LANES
    t = jnp.linspace(0.0, 1.0, L, dtype=F32)[:, None]
    w = (2.0 * math.pi / L) * jnp.arange(L, dtype=F32)[:, None]
    bands = jnp.linspace(1e-4, HY_BANDS - 1, HY_BANDS, dtype=F32)[None, :]
    feats = jnp.concatenate([t, jnp.cos(bands * w), -jnp.sin(bands * w),
                             jnp.zeros((L, hp - HY_EMB), F32)], axis=-1)

    def pad2(m, r, c):
        return jnp.pad(m.astype(F32), ((0, r - m.shape[0]), (0, c - m.shape[1])))

    w1 = pad2(f_w1, hp, hp)
    w2 = pad2(f_w2, hp, hp)
    w3 = pad2(f_w3, hp, f_w3.shape[1])
    b1, fr1 = pad2(f_b1[None], 1, hp), pad2(f_fr1[None], 1, hp)
    b2, fr2 = pad2(f_b2[None], 1, hp), pad2(f_fr2[None], 1, hp)
    max_decay = math.log(HY_TARGET) / HY_FAST_DECAY
    min_decay = math.log(HY_TARGET) / HY_SLOW_DECAY
    deltas = jnp.abs(jnp.linspace(min_decay, max_decay, d, dtype=F32))[None, :]
    ns = HY_ORDER * 2
    tl = _row_tile(L, tl)
    vec = lambda n: pl.BlockSpec((1, n), lambda s, i: (0, 0))
    mat = lambda r, c: pl.BlockSpec((r, c), lambda s, i: (0, 0))
    return pl.pallas_call(
        functools.partial(_filter_kernel, seq_len=L),
        grid=(ns, L // tl),
        in_specs=[pl.BlockSpec((tl, hp), lambda s, i: (i, 0)),
                  mat(hp, hp), vec(hp), vec(hp), mat(hp, hp), vec(hp), vec(hp),
                  pl.BlockSpec((hp, d), lambda s, i: (0, s)),
                  pl.BlockSpec((1, d), lambda s, i: (0, s)),
                  vec(d)],
        out_specs=[pl.BlockSpec((None, tl, d), lambda s, i: (s, i, 0)),
                   pl.BlockSpec((None, 1, d), lambda s, i: (s, 0, 0))],
        out_shape=[jax.ShapeDtypeStruct((ns, L, d), BF16), jax.ShapeDtypeStruct((ns, 1, d), F32)],
        compiler_params=pltpu.CompilerParams(dimension_semantics=("parallel", "arbitrary")),
        name="hyena_filter_taps",
    )(feats, w1, b1, fr1, w2, b2, fr2, w3, f_b3[None].astype(F32), deltas)


def _fft_bf_kernel(yf_ref, yb_ref, sf_ref, sb_ref, mf_ref, h_ref):
    n2 = yf_ref.shape[1]
    tc = yf_ref.shape[2]
    zf = jnp.dot(mf_ref[...], yf_ref[...].reshape(2 * n2, tc), preferred_element_type=F32)
    zb = jnp.dot(mf_ref[...], yb_ref[...].reshape(2 * n2, tc), preferred_element_type=F32)
    sf = 1.0 / (sf_ref[...] + 1e-6)
    sb = 1.0 / (sb_ref[...] + 1e-6)
    h_ref[0] = zf[:n2] * sf + zb[:n2] * sb
    h_ref[1] = zf[n2:] * sf - zb[n2:] * sb


def hyena_spectrum(L, d, fa, tw, mf, *filter_params, tc=1024):
    taps, sums = hyena_filter_taps(L, d, *filter_params)
    yv = fft_stage_a(taps, fa, tw)
    _, _, n1, n2, _ = yv.shape
    tc = _row_tile(d, tc)
    side = lambda sd: pl.BlockSpec((None, 2, None, n2, tc), lambda o, k, j: (2 * o + sd, 0, k, 0, j))
    norm = lambda sd: pl.BlockSpec((None, 1, tc), lambda o, k, j: (2 * o + sd, 0, j))
    return pl.pallas_call(
        _fft_bf_kernel,
        grid=(HY_ORDER, n1, d // tc),
        in_specs=[side(0), side(1), norm(0), norm(1), pl.BlockSpec(mf.shape, lambda o, k, j: (0, 0))],
        out_specs=pl.BlockSpec((None, 2, None, n2, tc), lambda o, k, j: (o, 0, k, 0, j)),
        out_shape=jax.ShapeDtypeStruct((HY_ORDER, 2, n1, n2, d), F32),
        compiler_params=pltpu.CompilerParams(
            dimension_semantics=("parallel", "parallel", "parallel"), vmem_limit_bytes=VMEM_LIMIT),
        name="fft_stage_b_filter",
    )(yv, yv, sums, sums, mf)


def _shortconv_kernel(u_ref, up_ref, un_ref, w_ref, b_ref, v_ref, g1_ref, g2_ref):
    i = pl.program_id(1)
    tr = u_ref.shape[0]
    d = v_ref.shape[1]
    hr = up_ref.shape[0]
    first = i == 0
    last = i == pl.num_programs(1) - 1
    for j, o_ref in enumerate((v_ref, g1_ref, g2_ref)):
        cols = slice(j * d, (j + 1) * d)
        u = u_ref[:, cols].astype(F32)
        row = lax.broadcasted_iota(jnp.int32, u.shape, 0)
        prev_row = jnp.where(first, 0.0, up_ref[hr - 1:hr, cols].astype(F32))
        next_row = jnp.where(last, 0.0, un_ref[0:1, cols].astype(F32))
        um = jnp.where(row == 0, prev_row, pltpu.roll(u, 1, axis=0))
        up = jnp.where(row == tr - 1, next_row, pltpu.roll(u, tr - 1, axis=0))
        w = w_ref[:, cols]
        o_ref[...] = (um * w[0:1] + u * w[1:2] + up * w[2:3] + b_ref[:, cols]).astype(o_ref.dtype)


def hyena_shortconv(u, conv_w, conv_b, tr=256):
    b, L, d3 = u.shape
    d = d3 // 3
    tr = _row_tile(L, tr)
    hr = 16
    hb = tr // hr
    nhb = L // hr
    out = pl.BlockSpec((None, tr, d), lambda bi, i: (bi, i, 0))
    shape = jax.ShapeDtypeStruct((b, L, d), BF16)
    return pl.pallas_call(
        _shortconv_kernel,
        grid=(b, L // tr),
        in_specs=[pl.BlockSpec((None, tr, d3), lambda bi, i: (bi, i, 0)),
                  pl.BlockSpec((None, hr, d3), lambda bi, i: (bi, jnp.maximum(i * hb - 1, 0), 0)),
                  pl.BlockSpec((None, hr, d3), lambda bi, i: (bi, jnp.minimum((i + 1) * hb, nhb - 1), 0)),
                  pl.BlockSpec((3, d3), lambda bi, i: (0, 0)),
                  pl.BlockSpec((1, d3), lambda bi, i: (0, 0))],
        out_specs=[out, out, out],
        out_shape=[shape, shape, shape],
        compiler_params=pltpu.CompilerParams(dimension_semantics=("parallel", "parallel")),
        name="hyena_shortconv",
    )(u, u, u, conv_w.astype(F32), conv_b[None].astype(F32))


def hyena_block(x, norm_g, w_in, conv_w, conv_b, f_w1, f_b1, f_fr1, f_w2, f_b2, f_fr2, f_w3, f_b3,
                skip, w_out):
    b, L, d = x.shape
    t = b * L
    x2 = x.reshape(t, d)
    u = norm_matmul(x2, norm_g, w_in.astype(BF16), out_dtype=BF16).reshape(b, L, 3 * d)
    v, g1, g2 = hyena_shortconv(u, conv_w, conv_b)
    fa, gc, tw, mf, mi = _fft_tables(L)
    spec = hyena_spectrum(L, d, fa, tw, mf, f_w1, f_b1, f_fr1, f_w2, f_b2, f_fr2, f_w3, f_b3)
    z = fft_stage_c(fft_stage_b(fft_stage_a(v, fa, tw), spec, 0, mf, mi), gc, tw, v, g1, skip[0])
    y = fft_stage_c(fft_stage_b(fft_stage_a(z, fa, tw), spec, 1, mf, mi), gc, tw, z, g2, skip[1])
    out = matmul_residual(y.reshape(t, d), w_out.astype(BF16), x2)
    return out.reshape(b, L, d)


def _router_kernel(x_ref, g_ref, w_ref, lg_ref, h_ref):
    h = _rms(x_ref[...].astype(F32), g_ref[...]).astype(BF16)
    h_ref[...] = h
    lg_ref[...] = jnp.dot(h, w_ref[...], preferred_element_type=F32)


def _expert_kernel(be_ref, nu_ref, dst_ref, x_ref, wg_ref, wu_ref, wd_ref, rw_ref, o_hbm,
                   wg_s, wu_s, wd_s, y_s, sem, *, dump_row):
    i = pl.program_id(0)
    tm = x_ref.shape[0]
    used = i < nu_ref[0]
    slot = i % 2

    def block_rows(s):
        return pltpu.make_async_copy(y_s.at[s], o_hbm.at[pl.ds(dump_row, tm), :], sem.at[s])

    @pl.when(i == 0)
    def _():
        y_s[1] = jnp.zeros(y_s.shape[1:], y_s.dtype)
        block_rows(1).start()
        block_rows(1).wait()

    @pl.when(jnp.logical_or(i == 0, be_ref[i] != be_ref[jnp.maximum(i - 1, 0)]))
    def _():
        wg_s[...] = wg_ref[...].astype(BF16)
        wu_s[...] = wu_ref[...].astype(BF16)
        wd_s[...] = wd_ref[...].astype(BF16)

    @pl.when(used)
    def _():
        x = x_ref[...]
        g = jnp.dot(x, wg_s[...], preferred_element_type=F32)
        u = jnp.dot(x, wu_s[...], preferred_element_type=F32)
        a = (g * jax.nn.sigmoid(g) * u).astype(BF16)
        y = jnp.dot(a, wd_s[...], preferred_element_type=F32)
        y_s[slot] = y * rw_ref[...]

    @pl.when(jnp.logical_and(i >= 1, i - 1 < nu_ref[0]))
    def _():
        block_rows(1 - slot).wait()

    @pl.when(used)
    def _():
        for r in range(tm):
            pltpu.make_async_copy(y_s.at[slot, pl.ds(r, 1), :],
                                  o_hbm.at[pl.ds(dst_ref[0, r], 1), :], sem.at[slot]).start()

    @pl.when(jnp.logical_and(i == pl.num_programs(0) - 1, used))
    def _():
        block_rows(slot).wait()


def _moe_add_kernel(x_ref, a_ref, b_ref, o_ref):
    o_ref[...] = x_ref[...] + a_ref[...] + b_ref[...]


def moe_block(x, norm_g, w_rg, b_rg, w_re, b_re, w_gate, w_up, w_down, tm=256):
    nb, L, d = x.shape
    T = nb * L
    ne = N_EXPERTS
    x2 = x.reshape(T, d)
    nr = N_GROUPS + ne
    w_r = jnp.concatenate([w_rg, w_re, jnp.zeros((d, LANES - nr), w_rg.dtype)], axis=1).astype(BF16)
    tr = _row_tile(T, 512)
    logits, h = pl.pallas_call(
        _router_kernel,
        grid=(T // tr,),
        in_specs=[pl.BlockSpec((tr, d), lambda i: (i, 0)),
                  pl.BlockSpec((1, d), lambda i: (0, 0)),
                  pl.BlockSpec((d, LANES), lambda i: (0, 0))],
        out_specs=[pl.BlockSpec((tr, LANES), lambda i: (i, 0)),
                   pl.BlockSpec((tr, d), lambda i: (i, 0))],
        out_shape=[jax.ShapeDtypeStruct((T, LANES), F32), jax.ShapeDtypeStruct((T, d), BF16)],
        compiler_params=pltpu.CompilerParams(dimension_semantics=("parallel",)),
        name="moe_router",
    )(x2, norm_g.reshape(1, d).astype(F32), w_r)

    lg = logits[:, :N_GROUPS] + b_rg.astype(F32)
    pg = jax.nn.softmax(lg, axis=-1)
    gsel = jnp.argmax(lg, axis=-1).astype(jnp.int32)
    pg_sel = jnp.take_along_axis(pg, gsel[:, None], axis=1)
    le = (logits[:, N_GROUPS:nr] + b_re.astype(F32)).reshape(T, N_GROUPS, EXPERTS_PER_GROUP)
    le = jnp.take_along_axis(le, gsel[:, None, None], axis=1)[:, 0]
    pe = jax.nn.softmax(le, axis=-1)
    top_p, top_i = lax.top_k(pe, TOP_K)
    gate = pg_sel * top_p / jnp.sum(top_p, axis=-1, keepdims=True)
    eid = (gsel[:, None] * EXPERTS_PER_GROUP + top_i.astype(jnp.int32)).reshape(-1)
    gw = gate.reshape(-1)

    na = T * TOP_K
    order = jnp.argsort(eid).astype(jnp.int32)
    counts = jnp.bincount(eid, length=ne).astype(jnp.int32)
    start = jnp.cumsum(counts) - counts
    pcounts = (counts + tm - 1) // tm * tm
    pend = jnp.cumsum(pcounts)
    pstart = pend - pcounts
    n_rows = na + ne * tm
    n_blk = n_rows // tm
    blk_first = jnp.arange(n_blk, dtype=jnp.int32) * tm
    blk_e = jnp.minimum(jnp.sum((pend[None, :] <= blk_first[:, None]).astype(jnp.int32), axis=1), ne - 1)
    rep = lambda per_blk: jnp.repeat(per_blk, tm)
    rows = jnp.arange(n_rows, dtype=jnp.int32)
    off = rows - rep(pstart[blk_e])
    live = off < rep(counts[blk_e])
    row_a = order[jnp.clip(rep(start[blk_e]) + off, 0, na - 1)]
    row_tok = jnp.where(live, row_a // TOP_K, 0)
    row_w = jnp.where(live, gw[row_a], 0.0)
    row_dst = jnp.where(live, (row_a % TOP_K) * T + row_a // TOP_K, TOP_K * T + rows % tm)
    n_used = (pend[-1:] // tm).astype(jnp.int32)

    xs = h[row_tok]
    de = w_gate.shape[-1]
    y = pl.pallas_call(
        functools.partial(_expert_kernel, dump_row=TOP_K * T),
        grid_spec=pltpu.PrefetchScalarGridSpec(
            num_scalar_prefetch=2,
            grid=(n_blk,),
            in_specs=[pl.BlockSpec((None, 1, tm), lambda i, be, nu: (i, 0, 0), memory_space=pltpu.SMEM),
                      pl.BlockSpec((tm, d), lambda i, be, nu: (i, 0)),
                      pl.BlockSpec((None, d, de), lambda i, be, nu: (be[i], 0, 0)),
                      pl.BlockSpec((None, d, de), lambda i, be, nu: (be[i], 0, 0)),
                      pl.BlockSpec((None, de, d), lambda i, be, nu: (be[i], 0, 0)),
                      pl.BlockSpec((tm, 1), lambda i, be, nu: (i, 0))],
            out_specs=pl.BlockSpec(memory_space=pl.ANY),
            scratch_shapes=[pltpu.VMEM((d, de), BF16), pltpu.VMEM((d, de), BF16),
                            pltpu.VMEM((de, d), BF16), pltpu.VMEM((2, tm, d), F32),
                            pltpu.SemaphoreType.DMA((2,))]),
        out_shape=jax.ShapeDtypeStruct((TOP_K * T + tm, d), F32),
        compiler_params=pltpu.CompilerParams(
            dimension_semantics=("arbitrary",), vmem_limit_bytes=VMEM_LIMIT),
        name="moe_experts",
    )(blk_e, n_used, row_dst.reshape(n_blk, 1, tm), xs, w_gate, w_up, w_down, row_w[:, None])

    assert TOP_K == 2
    ta = _row_tile(T, 512)
    blk = lambda shift: pl.BlockSpec((ta, d), lambda i: (i + shift, 0))
    out = pl.pallas_call(
        _moe_add_kernel,
        grid=(T // ta,),
        in_specs=[blk(0), blk(0), blk(T // ta)],
        out_specs=blk(0),
        out_shape=jax.ShapeDtypeStruct((T, d), F32),
        compiler_params=pltpu.CompilerParams(dimension_semantics=("parallel",)),
        name="moe_combine",
    )(x2, y, y)
    return out.reshape(nb, L, d)


def encoder_trunk(x, norm_mix, norm_ffn, norm_final, mla_p, hy_p, moe_p):
    for i in range(DEPTH):
        j = i // 2
        if i % 2 == 0:
            x = mla_block(x, norm_mix[i], *[p[j] for p in mla_p])
        else:
            x = hyena_block(x, norm_mix[i], *[p[j] for p in hy_p])
        x = moe_block(x, norm_ffn[i], *[p[i] for p in moe_p])
    return rms_norm_pallas(x, norm_final)


def kernel(x_prompt, x_sample, norm_mix, norm_ffn, norm_final, mla_w_in, mla_q_norm, mla_w_q_up, mla_kv_norm, mla_w_kv_up, mla_w_out, hy_w_in, hy_conv_w, hy_conv_b, hy_f_w1, hy_f_b1, hy_f_freq1, hy_f_w2, hy_f_b2, hy_f_freq2, hy_f_w3, hy_f_b3, hy_skip, hy_w_out, moe_w_group, moe_b_group, moe_w_expert, moe_b_expert, moe_w_gate, moe_w_up, moe_w_down):
    mla_p = (mla_w_in, mla_q_norm, mla_w_q_up, mla_kv_norm, mla_w_kv_up, mla_w_out)
    hy_p = (hy_w_in, hy_conv_w, hy_conv_b, hy_f_w1, hy_f_b1, hy_f_freq1, hy_f_w2, hy_f_b2,
            hy_f_freq2, hy_f_w3, hy_f_b3, hy_skip, hy_w_out)
    moe_p = (moe_w_group, moe_b_group, moe_w_expert, moe_b_expert, moe_w_gate, moe_w_up, moe_w_down)
    y_prompt = encoder_trunk(x_prompt, norm_mix, norm_ffn, norm_final, mla_p, hy_p, moe_p)
    y_sample = encoder_trunk(x_sample, norm_mix, norm_ffn, norm_final, mla_p, hy_p, moe_p)
    return (y_prompt, y_sample)
```

```python
import functools
import math
import jax
import jax.numpy as jnp
from jax import lax
from jax.experimental import pallas as pl
from jax.experimental.pallas import tpu as pltpu

D_MODEL = 2048
DEPTH = 2
N_HEADS = 16
Q_RANK = 512
KV_RANK = 512
QK_NOPE = 128
QK_ROPE = 64
V_DIM = 128
ROPE_THETA = 10000.0
ATTN_SCALE = 1.0 / math.sqrt(QK_NOPE + QK_ROPE)
HY_ORDER = 2
HY_EMB = 33
HY_BANDS = (HY_EMB - 1) // 2
HY_HIDDEN = 64
HY_FAST_DECAY = 0.3
HY_SLOW_DECAY = 1.5
HY_TARGET = 1e-2
N_GROUPS = 8
EXPERTS_PER_GROUP = 8
N_EXPERTS = N_GROUPS * EXPERTS_PER_GROUP
TOP_K = 2
D_EXPERT = 512
ROW_BLOCK = 128
NORM_EPS = 1e-6

F32 = jnp.float32
BF16 = jnp.bfloat16
LANES = 128
HEAD_W = 2 * LANES
VMEM_LIMIT = 56 * 1024 * 1024
FFT_GROUP = 16


def _row_tile(n, want):
    t = min(n, want)
    while n % t:
        t //= 2
    return t


def _rms(x, g):
    return x * lax.rsqrt(jnp.mean(x * x, axis=-1, keepdims=True) + NORM_EPS) * g


def _norm_mm_kernel(x_ref, g_ref, w_ref, o_ref, h_scr):
    @pl.when(pl.program_id(1) == 0)
    def _():
        h_scr[...] = _rms(x_ref[...].astype(F32), g_ref[...]).astype(BF16)

    o_ref[...] = jnp.dot(h_scr[...], w_ref[...], preferred_element_type=F32).astype(o_ref.dtype)


def norm_matmul(x, g, w, out_dtype=F32, tm=512, tn=1024):
    m, k = x.shape
    n = w.shape[1]
    tm = _row_tile(m, tm)
    tn = n if n <= 2048 else _row_tile(n, tn)
    return pl.pallas_call(
        _norm_mm_kernel,
        grid=(m // tm, n // tn),
        in_specs=[pl.BlockSpec((tm, k), lambda i, j: (i, 0)),
                  pl.BlockSpec((1, k), lambda i, j: (0, 0)),
                  pl.BlockSpec((k, tn), lambda i, j: (0, j))],
        out_specs=pl.BlockSpec((tm, tn), lambda i, j: (i, j)),
        out_shape=jax.ShapeDtypeStruct((m, n), out_dtype),
        scratch_shapes=[pltpu.VMEM((tm, k), BF16)],
        compiler_params=pltpu.CompilerParams(
            dimension_semantics=("parallel", "arbitrary"), vmem_limit_bytes=VMEM_LIMIT),
        name="norm_matmul",
    )(x, g.reshape(1, k).astype(F32), w)


def _mm_res_kernel(a_ref, w_ref, r_ref, o_ref):
    acc = jnp.dot(a_ref[...].astype(BF16), w_ref[...], preferred_element_type=F32)
    o_ref[...] = (r_ref[...].astype(F32) + acc).astype(o_ref.dtype)


def matmul_residual(a, w, r, tm=512, tn=1024):
    m, k = a.shape
    n = w.shape[1]
    tm = _row_tile(m, tm)
    tn = _row_tile(n, tn)
    return pl.pallas_call(
        _mm_res_kernel,
        grid=(m // tm, n // tn),
        in_specs=[pl.BlockSpec((tm, k), lambda i, j: (i, 0)),
                  pl.BlockSpec((k, tn), lambda i, j: (0, j)),
                  pl.BlockSpec((tm, tn), lambda i, j: (i, j))],
        out_specs=pl.BlockSpec((tm, tn), lambda i, j: (i, j)),
        out_shape=jax.ShapeDtypeStruct((m, n), r.dtype),
        compiler_params=pltpu.CompilerParams(
            dimension_semantics=("parallel", "parallel"), vmem_limit_bytes=VMEM_LIMIT),
        name="matmul_residual",
    )(a, w, r)


def _rms_kernel(x_ref, g_ref, o_ref):
    o_ref[...] = _rms(x_ref[...].astype(F32), g_ref[...]).astype(o_ref.dtype)


def rms_norm_pallas(x, g, rows=512):
    shp = x.shape
    d = shp[-1]
    x2 = x.reshape(-1, d)
    t = x2.shape[0]
    rows = _row_tile(t, rows)
    out = pl.pallas_call(
        _rms_kernel,
        grid=(t // rows,),
        in_specs=[pl.BlockSpec((rows, d), lambda i: (i, 0)),
                  pl.BlockSpec((1, d), lambda i: (0, 0))],
        out_specs=pl.BlockSpec((rows, d), lambda i: (i, 0)),
        out_shape=jax.ShapeDtypeStruct((t, d), x.dtype),
        compiler_params=pltpu.CompilerParams(dimension_semantics=("parallel",)),
        name="rms_norm",
    )(x2, g.reshape(1, d).astype(F32))
    return out.reshape(shp)


def _mla_qkv_kernel(c_ref, qg_ref, kvg_ref, wq_ref, wkv_ref, cq_ref, sq_ref, ck_ref, sk_ref,
                    q_ref, k_ref, v_ref, *, n_heads, q_rank, kv_rank):
    c = c_ref[...]
    cqn = _rms(c[:, :q_rank], qg_ref[...]).astype(BF16)
    ckvn = _rms(c[:, q_rank:q_rank + kv_rank], kvg_ref[...]).astype(BF16)
    kr = c[:, q_rank + kv_rank:]
    k_hi = (kr * ck_ref[...] + pltpu.roll(kr, LANES // 2, axis=1) * sk_ref[...]).astype(BF16)
    tr = c.shape[0]
    lane = lax.broadcasted_iota(jnp.int32, (tr, LANES), 1)
    v_hi = jnp.where(lane == 0, 1.0, 0.0).astype(BF16)
    cq = cq_ref[...]
    sq = sq_ref[...]
    qw = 3 * LANES
    for h in range(n_heads):
        a = jnp.dot(cqn, wq_ref[:, h * qw:(h + 1) * qw], preferred_element_type=F32)
        q_ref[h, :, :LANES] = (a[:, :LANES] * ATTN_SCALE).astype(BF16)
        q_ref[h, :, LANES:] = (a[:, LANES:2 * LANES] * cq + a[:, 2 * LANES:] * sq).astype(BF16)
        kv = jnp.dot(ckvn, wkv_ref[:, h * HEAD_W:(h + 1) * HEAD_W], preferred_element_type=F32)
        k_ref[h, :, :LANES] = kv[:, :LANES].astype(BF16)
        k_ref[h, :, LANES:] = k_hi
        v_ref[h, :, :LANES] = kv[:, LANES:].astype(BF16)
        v_ref[h, :, LANES:] = v_hi


def _attn_kernel(q_ref, k_ref, v_ref, o_ref, *, tk, unroll):
    q = q_ref[...]
    tq = q.shape[0]
    nk = k_ref.shape[0] // tk

    def body(j, carry):
        m, acc = carry
        off = pl.multiple_of(j * tk, tk)
        s = lax.dot_general(q, k_ref[pl.ds(off, tk), :], (((1,), (1,)), ((), ())),
                            preferred_element_type=F32)
        m_new = jnp.maximum(m, jnp.max(s, axis=1, keepdims=True))
        p = jnp.exp((s - m_new).astype(BF16))
        acc = acc * jnp.exp(m - m_new) + jnp.dot(p, v_ref[pl.ds(off, tk), :],
                                                 preferred_element_type=F32)
        return m_new, acc

    m0 = jnp.full((tq, 1), -1e30, F32)
    acc0 = jnp.zeros((tq, HEAD_W), F32)
    _, acc = lax.fori_loop(0, nk, body, (m0, acc0), unroll=min(unroll, nk))
    o_ref[...] = (acc[:, :LANES] / acc[:, LANES:LANES + 1]).astype(o_ref.dtype)


def mla_attention(q, k, v, tq=512, tk=1024, unroll=16):
    b, nh, L, _ = q.shape
    tq = _row_tile(L, tq)
    tk = _row_tile(L, tk)
    kv_spec = pl.BlockSpec((None, None, L, HEAD_W), lambda bi, h, i: (bi, h, 0, 0))
    return pl.pallas_call(
        functools.partial(_attn_kernel, tk=tk, unroll=unroll),
        grid=(b, nh, L // tq),
        in_specs=[pl.BlockSpec((None, None, tq, HEAD_W), lambda bi, h, i: (bi, h, i, 0)),
                  kv_spec, kv_spec],
        out_specs=pl.BlockSpec((None, tq, V_DIM), lambda bi, h, i: (bi, i, h)),
        out_shape=jax.ShapeDtypeStruct((b, L, nh * V_DIM), BF16),
        compiler_params=pltpu.CompilerParams(
            dimension_semantics=("parallel", "parallel", "arbitrary"),
            vmem_limit_bytes=VMEM_LIMIT),
        name="mla_attention",
    )(q, k, v)


def _rot_cols(w):
    half = w.shape[-1] // 2
    return jnp.concatenate([-w[..., half:], w[..., :half]], axis=-1)


def mla_block(x, norm_g, w_in, q_norm, w_q_up, kv_norm, w_kv_up, w_out, tq=512, tk=1024):
    assert QK_NOPE == LANES and V_DIM == LANES and QK_ROPE == LANES // 2
    b, L, d = x.shape
    t = b * L
    nh = N_HEADS
    x2 = x.reshape(t, d)
    rk = Q_RANK + KV_RANK
    w_in_ext = jnp.concatenate([w_in, _rot_cols(w_in[:, rk:])], axis=1).astype(BF16)
    c = norm_matmul(x2, norm_g, w_in_ext)

    wq = w_q_up.reshape(Q_RANK, nh, QK_NOPE + QK_ROPE)
    rope_w = wq[..., QK_NOPE:]
    z = jnp.zeros_like(rope_w)
    wq3 = jnp.concatenate([wq[..., :QK_NOPE], rope_w, z, _rot_cols(rope_w), z], axis=-1)
    wq3 = wq3.reshape(Q_RANK, nh * 3 * LANES).astype(BF16)
    wkv = w_kv_up.astype(BF16)

    inv = 1.0 / (ROPE_THETA ** (jnp.arange(0, QK_ROPE, 2, dtype=F32) / QK_ROPE))
    ang = jnp.arange(L, dtype=F32)[:, None] * inv[None, :]
    ang = jnp.concatenate([ang, ang], axis=-1)
    zpad = jnp.zeros((L, LANES - QK_ROPE), F32)
    cos_k = jnp.concatenate([jnp.cos(ang), zpad], axis=1)
    sin_k = jnp.concatenate([jnp.sin(ang), zpad], axis=1)
    cos_q = cos_k * ATTN_SCALE
    sin_q = sin_k * ATTN_SCALE

    tr = _row_tile(L, 256)
    nl = L // tr
    cw = c.shape[1]
    tab = pl.BlockSpec((tr, LANES), lambda i: (i % nl, 0))
    slab = pl.BlockSpec((None, nh, tr, HEAD_W), lambda i: (i // nl, 0, i % nl, 0))
    slab_shape = jax.ShapeDtypeStruct((b, nh, L, HEAD_W), BF16)
    q, k, v = pl.pallas_call(
        functools.partial(_mla_qkv_kernel, n_heads=nh, q_rank=Q_RANK, kv_rank=KV_RANK),
        grid=(t // tr,),
        in_specs=[pl.BlockSpec((tr, cw), lambda i: (i, 0)),
                  pl.BlockSpec((1, Q_RANK), lambda i: (0, 0)),
                  pl.BlockSpec((1, KV_RANK), lambda i: (0, 0)),
                  pl.BlockSpec(wq3.shape, lambda i: (0, 0)),
                  pl.BlockSpec(wkv.shape, lambda i: (0, 0)),
                  tab, tab, tab, tab],
        out_specs=[slab, slab, slab],
        out_shape=[slab_shape, slab_shape, slab_shape],
        compiler_params=pltpu.CompilerParams(
            dimension_semantics=("parallel",), vmem_limit_bytes=VMEM_LIMIT),
        name="mla_qkv",
    )(c, q_norm.reshape(1, -1).astype(F32), kv_norm.reshape(1, -1).astype(F32), wq3, wkv,
      cos_q, sin_q, cos_k, sin_k)

    o = mla_attention(q, k, v, tq=tq, tk=tk)

    out = matmul_residual(o.reshape(t, nh * V_DIM), w_out.astype(BF16), x2)
    return out.reshape(b, L, d)


def _fft_split(L):
    n = 2 * L
    n2 = min(256, n // 16)
    return n // n2, n2


def _fft_tables(L):
    n1, n2 = _fft_split(L)
    n = n1 * n2
    n1h = n1 // 2
    k1n = n1 // 2 + 1
    k1p = min(n1, -(-k1n // 8) * 8)
    g = FFT_GROUP
    nb = n2 // g
    k1 = jnp.arange(k1p, dtype=jnp.int32)
    m1 = jnp.arange(n1h, dtype=jnp.int32)
    m2 = jnp.arange(n2, dtype=jnp.int32)
    live = (k1 < k1n).astype(F32)
    wgt = jnp.where((k1 == 0) | (k1 == n1 // 2), 1.0, 2.0) * live / n1
    th1 = ((k1[:, None] * m1[None, :]) % n1).astype(F32) * (2.0 * math.pi / n1)
    c1, s1 = jnp.cos(th1), jnp.sin(th1)
    eye = jnp.eye(g, dtype=F32)
    fa = jnp.concatenate([jnp.kron(c1 * live[:, None], eye), jnp.kron(-s1 * live[:, None], eye)],
                         axis=0).astype(BF16)
    gc = jnp.concatenate([jnp.kron((c1 * wgt[:, None]).T, eye), jnp.kron((-s1 * wgt[:, None]).T, eye)],
                         axis=1).astype(BF16)
    tht = ((k1[:, None] * m2[None, :]) % n).astype(F32) * (2.0 * math.pi / n)
    tw = jnp.stack([jnp.cos(tht), -jnp.sin(tht)], axis=0).reshape(2, k1p, nb, g)
    tw = tw.transpose(0, 2, 1, 3).reshape(2, nb, k1p * g, 1)
    idx2 = (m2[:, None] * m2[None, :]) % n2
    th2 = idx2.astype(F32) * (2.0 * math.pi / n2)
    c2, s2 = jnp.cos(th2), jnp.sin(th2)
    mf = jnp.concatenate([jnp.concatenate([c2, s2], 1), jnp.concatenate([-s2, c2], 1)], 0).astype(BF16)
    mi = (jnp.concatenate([jnp.concatenate([c2, -s2], 1), jnp.concatenate([s2, c2], 1)], 0) / n2).astype(BF16)
    return fa, gc, tw, mf, mi


def _fft_a_kernel(x_ref, fa_ref, tw_ref, y_ref):
    n1h, g, tc = x_ref.shape
    y = jnp.dot(fa_ref[...], x_ref[...].reshape(n1h * g, tc), preferred_element_type=F32)
    half = y.shape[0] // 2
    yr, yi = y[:half], y[half:]
    tr, ti = tw_ref[0], tw_ref[1]
    y_ref[0] = (yr * tr - yi * ti).reshape(y_ref.shape[1:]).astype(y_ref.dtype)
    y_ref[1] = (yr * ti + yi * tr).reshape(y_ref.shape[1:]).astype(y_ref.dtype)


def fft_stage_a(x, fa, tw, tc=512):
    s, L, d = x.shape
    rows, cols = fa.shape
    nb = tw.shape[1]
    g = FFT_GROUP
    n1h, k1p, n2 = cols // g, rows // (2 * g), nb * g
    tc = _row_tile(d, tc)
    return pl.pallas_call(
        _fft_a_kernel,
        grid=(s, nb, d // tc),
        in_specs=[pl.BlockSpec((None, n1h, g, tc), lambda b, j, c: (b, 0, j, c)),
                  pl.BlockSpec((rows, cols), lambda b, j, c: (0, 0)),
                  pl.BlockSpec((2, None, k1p * g, 1), lambda b, j, c: (0, j, 0, 0))],
        out_specs=pl.BlockSpec((None, 2, k1p, g, tc), lambda b, j, c: (b, 0, 0, j, c)),
        out_shape=jax.ShapeDtypeStruct((s, 2, k1p, n2, d), BF16),
        compiler_params=pltpu.CompilerParams(
            dimension_semantics=("parallel", "parallel", "parallel"), vmem_limit_bytes=VMEM_LIMIT),
        name="fft_stage_a",
    )(x.reshape(s, n1h, n2, d), fa, tw)


def _fft_b_kernel(y_ref, h_ref, mf_ref, mi_ref, u_ref):
    n2 = y_ref.shape[1]
    tc = y_ref.shape[2]
    z = jnp.dot(mf_ref[...], y_ref[...].reshape(2 * n2, tc), preferred_element_type=F32)
    zr, zi = z[:n2], z[n2:]
    hr, hi = h_ref[0], h_ref[1]
    w = jnp.concatenate([zr * hr - zi * hi, zr * hi + zi * hr], axis=0).astype(BF16)
    u = jnp.dot(mi_ref[...], w, preferred_element_type=F32)
    u_ref[0] = u[:n2].astype(u_ref.dtype)
    u_ref[1] = u[n2:].astype(u_ref.dtype)


def fft_stage_b(y, spec, order, mf, mi, tc=1024):
    s, _, k1p, n2, d = y.shape
    tc = _row_tile(d, tc)
    blk = pl.BlockSpec((None, 2, None, n2, tc), lambda k, j, b: (b, 0, k, 0, j))
    return pl.pallas_call(
        _fft_b_kernel,
        grid=(k1p, d // tc, s),
        in_specs=[blk,
                  pl.BlockSpec((None, 2, None, n2, tc), lambda k, j, b: (order, 0, k, 0, j)),
                  pl.BlockSpec(mf.shape, lambda k, j, b: (0, 0)),
                  pl.BlockSpec(mi.shape, lambda k, j, b: (0, 0))],
        out_specs=blk,
        out_shape=jax.ShapeDtypeStruct(y.shape, BF16),
        compiler_params=pltpu.CompilerParams(
            dimension_semantics=("parallel", "parallel", "arbitrary"), vmem_limit_bytes=VMEM_LIMIT),
        name="fft_stage_b",
    )(y, spec, mf, mi)


def _fft_c_kernel(u_ref, gc_ref, tw_ref, v_ref, g_ref, sk_ref, o_ref):
    _, k1p, g, tc = u_ref.shape
    rows = v_ref.shape[0] * g
    ur = u_ref[0].reshape(k1p * g, tc).astype(F32)
    ui = u_ref[1].reshape(k1p * g, tc).astype(F32)
    tr, ti = tw_ref[0], tw_ref[1]
    w = jnp.concatenate([ur * tr + ui * ti, ui * tr - ur * ti], axis=0).astype(BF16)
    y = jnp.dot(gc_ref[...], w, preferred_element_type=F32)
    v = v_ref[...].reshape(rows, tc).astype(F32)
    gate = g_ref[...].reshape(rows, tc).astype(F32)
    o_ref[...] = (gate * (y + v * sk_ref[...])).astype(o_ref.dtype).reshape(o_ref.shape)


def fft_stage_c(u, gc, tw, v, gate, skip, tc=512):
    s, _, k1p, n2, d = u.shape
    rows, cols = gc.shape
    g = FFT_GROUP
    nb = n2 // g
    n1h = rows // g
    L = n1h * n2
    tc = _row_tile(d, tc)
    row = pl.BlockSpec((None, n1h, g, tc), lambda b, j, c: (b, 0, j, c))
    out = pl.pallas_call(
        _fft_c_kernel,
        grid=(s, nb, d // tc),
        in_specs=[pl.BlockSpec((None, 2, k1p, g, tc), lambda b, j, c: (b, 0, 0, j, c)),
                  pl.BlockSpec((rows, cols), lambda b, j, c: (0, 0)),
                  pl.BlockSpec((2, None, k1p * g, 1), lambda b, j, c: (0, j, 0, 0)),
                  row, row,
                  pl.BlockSpec((1, tc), lambda b, j, c: (0, c))],
        out_specs=row,
        out_shape=jax.ShapeDtypeStruct((s, n1h, n2, d), BF16),
        compiler_params=pltpu.CompilerParams(
            dimension_semantics=("parallel", "parallel", "parallel"), vmem_limit_bytes=VMEM_LIMIT),
        name="fft_stage_c",
    )(u, gc, tw, v.reshape(s, n1h, n2, d), gate.reshape(s, n1h, n2, d), skip.reshape(1, d).astype(F32))
    return out.reshape(s, L, d)


def _filter_kernel(ft_ref, w1_ref, b1_ref, f1_ref, w2_ref, b2_ref, f2_ref, w3_ref, b3_ref, dl_ref,
                   taps_ref, sum_ref, *, seq_len):
    i = pl.program_id(1)
    tl = ft_ref.shape[0]
    a = jnp.sin(f1_ref[...] * (jnp.dot(ft_ref[...], w1_ref[...], preferred_element_type=F32) + b1_ref[...]))
    a = jnp.sin(f2_ref[...] * (jnp.dot(a, w2_ref[...], preferred_element_type=F32) + b2_ref[...]))
    f = jnp.dot(a, w3_ref[...], preferred_element_type=F32) + b3_ref[...]
    pos = (lax.broadcasted_iota(jnp.int32, (tl, 1), 0) + i * tl).astype(F32)
    t = pos * (1.0 / max(seq_len - 1, 1))
    f = f * jnp.exp(-t * dl_ref[...])

    @pl.when(i == 0)
    def _():
        sum_ref[...] = jnp.zeros_like(sum_ref)

    sum_ref[...] += jnp.sum(jnp.abs(f), axis=0, keepdims=True)
    is_bwd = pl.program_id(0) % 2 == 1
    f = jnp.where(jnp.logical_and(is_bwd, pos == 0.0), 0.0, f)
    taps_ref[...] = f.astype(taps_ref.dtype)


def hyena_filter_taps(L, d, f_w1, f_b1, f_fr1, f_w2, f_b2, f_fr2, f_w3, f_b3, tl=512):
    hp = LANES
    t = jnp.linspace(0.0, 1.0, L, dtype=F32)[:, None]
    w = (2.0 * math.pi / L) * jnp.arange(L, dtype=F32)[:, None]
    bands = jnp.linspace(1e-4, HY_BANDS - 1, HY_BANDS, dtype=F32)[None, :]
    feats = jnp.concatenate([t, jnp.cos(bands * w), -jnp.sin(bands * w),
                             jnp.zeros((L, hp - HY_EMB), F32)], axis=-1)

    def pad2(m, r, c):
        return jnp.pad(m.astype(F32), ((0, r - m.shape[0]), (0, c - m.shape[1])))

    w1 = pad2(f_w1, hp, hp)
    w2 = pad2(f_w2, hp, hp)
    w3 = pad2(f_w3, hp, f_w3.shape[1])
    b1, fr1 = pad2(f_b1[None], 1, hp), pad2(f_fr1[None], 1, hp)
    b2, fr2 = pad2(f_b2[None], 1, hp), pad2(f_fr2[None], 1, hp)
    max_decay = math.log(HY_TARGET) / HY_FAST_DECAY
    min_decay = math.log(HY_TARGET) / HY_SLOW_DECAY
    deltas = jnp.abs(jnp.linspace(min_decay, max_decay, d, dtype=F32))[None, :]
    ns = HY_ORDER * 2
    tl = _row_tile(L, tl)
    vec = lambda n: pl.BlockSpec((1, n), lambda s, i: (0, 0))
    mat = lambda r, c: pl.BlockSpec((r, c), lambda s, i: (0, 0))
    return pl.pallas_call(
        functools.partial(_filter_kernel, seq_len=L),
        grid=(ns, L // tl),
        in_specs=[pl.BlockSpec((tl, hp), lambda s, i: (i, 0)),
                  mat(hp, hp), vec(hp), vec(hp), mat(hp, hp), vec(hp), vec(hp),
                  pl.BlockSpec((hp, d), lambda s, i: (0, s)),
                  pl.BlockSpec((1, d), lambda s, i: (0, s)),
                  vec(d)],
        out_specs=[pl.BlockSpec((None, tl, d), lambda s, i: (s, i, 0)),
                   pl.BlockSpec((None, 1, d), lambda s, i: (s, 0, 0))],
        out_shape=[jax.ShapeDtypeStruct((ns, L, d), BF16), jax.ShapeDtypeStruct((ns, 1, d), F32)],
        compiler_params=pltpu.CompilerParams(dimension_semantics=("parallel", "arbitrary")),
        name="hyena_filter_taps",
    )(feats, w1, b1, fr1, w2, b2, fr2, w3, f_b3[None].astype(F32), deltas)


def _fft_bf_kernel(yf_ref, yb_ref, sf_ref, sb_ref, mf_ref, h_ref):
    n2 = yf_ref.shape[1]
    tc = yf_ref.shape[2]
    zf = jnp.dot(mf_ref[...], yf_ref[...].reshape(2 * n2, tc), preferred_element_type=F32)
    zb = jnp.dot(mf_ref[...], yb_ref[...].reshape(2 * n2, tc), preferred_element_type=F32)
    sf = 1.0 / (sf_ref[...] + 1e-6)
    sb = 1.0 / (sb_ref[...] + 1e-6)
    h_ref[0] = zf[:n2] * sf + zb[:n2] * sb
    h_ref[1] = zf[n2:] * sf - zb[n2:] * sb


def hyena_spectrum(L, d, fa, tw, mf, *filter_params, tc=1024):
    taps, sums = hyena_filter_taps(L, d, *filter_params)
    yv = fft_stage_a(taps, fa, tw)
    _, _, n1, n2, _ = yv.shape
    tc = _row_tile(d, tc)
    side = lambda sd: pl.BlockSpec((None, 2, None, n2, tc), lambda o, k, j: (2 * o + sd, 0, k, 0, j))
    norm = lambda sd: pl.BlockSpec((None, 1, tc), lambda o, k, j: (2 * o + sd, 0, j))
    return pl.pallas_call(
        _fft_bf_kernel,
        grid=(HY_ORDER, n1, d // tc),
        in_specs=[side(0), side(1), norm(0), norm(1), pl.BlockSpec(mf.shape, lambda o, k, j: (0, 0))],
        out_specs=pl.BlockSpec((None, 2, None, n2, tc), lambda o, k, j: (o, 0, k, 0, j)),
        out_shape=jax.ShapeDtypeStruct((HY_ORDER, 2, n1, n2, d), F32),
        compiler_params=pltpu.CompilerParams(
            dimension_semantics=("parallel", "parallel", "parallel"), vmem_limit_bytes=VMEM_LIMIT),
        name="fft_stage_b_filter",
    )(yv, yv, sums, sums, mf)


def _shortconv_kernel(u_ref, up_ref, un_ref, w_ref, b_ref, v_ref, g1_ref, g2_ref):
    i = pl.program_id(1)
    tr = u_ref.shape[0]
    d = v_ref.shape[1]
    hr = up_ref.shape[0]
    first = i == 0
    last = i == pl.num_programs(1) - 1
    for j, o_ref in enumerate((v_ref, g1_ref, g2_ref)):
        cols = slice(j * d, (j + 1) * d)
        u = u_ref[:, cols].astype(F32)
        row = lax.broadcasted_iota(jnp.int32, u.shape, 0)
        prev_row = jnp.where(first, 0.0, up_ref[hr - 1:hr, cols].astype(F32))
        next_row = jnp.where(last, 0.0, un_ref[0:1, cols].astype(F32))
        um = jnp.where(row == 0, prev_row, pltpu.roll(u, 1, axis=0))
        up = jnp.where(row == tr - 1, next_row, pltpu.roll(u, tr - 1, axis=0))
        w = w_ref[:, cols]
        o_ref[...] = (um * w[0:1] + u * w[1:2] + up * w[2:3] + b_ref[:, cols]).astype(o_ref.dtype)


def hyena_shortconv(u, conv_w, conv_b, tr=256):
    b, L, d3 = u.shape
    d = d3 // 3
    tr = _row_tile(L, tr)
    hr = 16
    hb = tr // hr
    nhb = L // hr
    out = pl.BlockSpec((None, tr, d), lambda bi, i: (bi, i, 0))
    shape = jax.ShapeDtypeStruct((b, L, d), BF16)
    return pl.pallas_call(
        _shortconv_kernel,
        grid=(b, L // tr),
        in_specs=[pl.BlockSpec((None, tr, d3), lambda bi, i: (bi, i, 0)),
                  pl.BlockSpec((None, hr, d3), lambda bi, i: (bi, jnp.maximum(i * hb - 1, 0), 0)),
                  pl.BlockSpec((None, hr, d3), lambda bi, i: (bi, jnp.minimum((i + 1) * hb, nhb - 1), 0)),
                  pl.BlockSpec((3, d3), lambda bi, i: (0, 0)),
                  pl.BlockSpec((1, d3), lambda bi, i: (0, 0))],
        out_specs=[out, out, out],
        out_shape=[shape, shape, shape],
        compiler_params=pltpu.CompilerParams(dimension_semantics=("parallel", "parallel")),
        name="hyena_shortconv",
    )(u, u, u, conv_w.astype(F32), conv_b[None].astype(F32))


def hyena_block(x, norm_g, w_in, conv_w, conv_b, f_w1, f_b1, f_fr1, f_w2, f_b2, f_fr2, f_w3, f_b3,
                skip, w_out):
    b, L, d = x.shape
    t = b * L
    x2 = x.reshape(t, d)
    u = norm_matmul(x2, norm_g, w_in.astype(BF16), out_dtype=BF16).reshape(b, L, 3 * d)
    v, g1, g2 = hyena_shortconv(u, conv_w, conv_b)
    fa, gc, tw, mf, mi = _fft_tables(L)
    spec = hyena_spectrum(L, d, fa, tw, mf, f_w1, f_b1, f_fr1, f_w2, f_b2, f_fr2, f_w3, f_b3)
    z = fft_stage_c(fft_stage_b(fft_stage_a(v, fa, tw), spec, 0, mf, mi), gc, tw, v, g1, skip[0])
    y = fft_stage_c(fft_stage_b(fft_stage_a(z, fa, tw), spec, 1, mf, mi), gc, tw, z, g2, skip[1])
    out = matmul_residual(y.reshape(t, d), w_out.astype(BF16), x2)
    return out.reshape(b, L, d)


def _router_kernel(x_ref, g_ref, w_ref, lg_ref, hp_ref):
    h = _rms(x_ref[...].astype(F32), g_ref[...]).astype(BF16)
    lg_ref[...] = jnp.dot(h, w_ref[...], preferred_element_type=F32)
    bits = lax.bitcast_convert_type(h.astype(F32), jnp.uint32)
    half = bits.shape[1] // 2
    hp_ref[...] = bits[:, :half] | (bits[:, half:] >> 16)


def _unpack_rows(packed):
    hi = lax.bitcast_convert_type(packed & jnp.uint32(0xFFFF0000), F32)
    lo = lax.bitcast_convert_type(packed << 16, F32)
    return jnp.concatenate([hi, lo], axis=1).astype(BF16)


def _expert_kernel(be_ref, nu_ref, dst_ref, tok_ref, tok_next_ref, hp_hbm, wg_ref, wu_ref, wd_ref, rw_ref,
                   o_hbm, wg_s, wu_s, wd_s, x_s, y_s, gsem, sem, *, dump_row):
    i = pl.program_id(0)
    tm = rw_ref.shape[0]
    used = i < nu_ref[0]
    slot = i % 2

    def block_rows(s):
        return pltpu.make_async_copy(y_s.at[s], o_hbm.at[pl.ds(dump_row, tm), :], sem.at[s])

    def gathered_rows(s):
        return pltpu.make_async_copy(hp_hbm.at[pl.ds(0, tm), :], x_s.at[s], gsem.at[s])

    def start_gather(idx_ref, s):
        for r in range(tm):
            pltpu.make_async_copy(hp_hbm.at[pl.ds(idx_ref[0, r], 1), :],
                                  x_s.at[s, pl.ds(r, 1), :], gsem.at[s]).start()

    @pl.when(jnp.logical_and(i == 0, used))
    def _():
        start_gather(tok_ref, 0)

    @pl.when(i + 1 < nu_ref[0])
    def _():
        start_gather(tok_next_ref, 1 - slot)

    @pl.when(i == 0)
    def _():
        y_s[1] = jnp.zeros(y_s.shape[1:], y_s.dtype)
        block_rows(1).start()
        block_rows(1).wait()

    @pl.when(jnp.logical_or(i == 0, be_ref[i] != be_ref[jnp.maximum(i - 1, 0)]))
    def _():
        wg_s[...] = wg_ref[...].astype(BF16)
        wu_s[...] = wu_ref[...].astype(BF16)
        wd_s[...] = wd_ref[...].astype(BF16)

    @pl.when(used)
    def _():
        gathered_rows(slot).wait()
        x = _unpack_rows(x_s[slot])
        g = jnp.dot(x, wg_s[...], preferred_element_type=F32)
        u = jnp.dot(x, wu_s[...], preferred_element_type=F32)
        a = (g * jax.nn.sigmoid(g) * u).astype(BF16)
        y = jnp.dot(a, wd_s[...], preferred_element_type=F32)
        y_s[slot] = y * rw_ref[...]

    @pl.when(jnp.logical_and(i >= 1, i - 1 < nu_ref[0]))
    def _():
        block_rows(1 - slot).wait()

    @pl.when(used)
    def _():
        for r in range(tm):
            pltpu.make_async_copy(y_s.at[slot, pl.ds(r, 1), :],
                                  o_hbm.at[pl.ds(dst_ref[0, r], 1), :], sem.at[slot]).start()

    @pl.when(jnp.logical_and(i == pl.num_programs(0) - 1, used))
    def _():
        block_rows(slot).wait()


def _moe_add_kernel(x_ref, a_ref, b_ref, o_ref):
    o_ref[...] = x_ref[...] + a_ref[...] + b_ref[...]


def moe_block(x, norm_g, w_rg, b_rg, w_re, b_re, w_gate, w_up, w_down, layer, tm=256):
    nb, L, d = x.shape
    T = nb * L
    ne = N_EXPERTS
    x2 = x.reshape(T, d)
    nr = N_GROUPS + ne
    w_r = jnp.concatenate([w_rg, w_re, jnp.zeros((d, LANES - nr), w_rg.dtype)], axis=1).astype(BF16)
    tr = _row_tile(T, 512)
    logits, hp = pl.pallas_call(
        _router_kernel,
        grid=(T // tr,),
        in_specs=[pl.BlockSpec((tr, d), lambda i: (i, 0)),
                  pl.BlockSpec((1, d), lambda i: (0, 0)),
                  pl.BlockSpec((d, LANES), lambda i: (0, 0))],
        out_specs=[pl.BlockSpec((tr, LANES), lambda i: (i, 0)),
                   pl.BlockSpec((tr, d // 2), lambda i: (i, 0))],
        out_shape=[jax.ShapeDtypeStruct((T, LANES), F32), jax.ShapeDtypeStruct((T, d // 2), jnp.uint32)],
        compiler_params=pltpu.CompilerParams(dimension_semantics=("parallel",)),
        name="moe_router",
    )(x2, norm_g.reshape(1, d).astype(F32), w_r)

    lg = logits[:, :N_GROUPS] + b_rg.astype(F32)
    pg = jax.nn.softmax(lg, axis=-1)
    gsel = jnp.argmax(lg, axis=-1).astype(jnp.int32)
    pg_sel = jnp.take_along_axis(pg, gsel[:, None], axis=1)
    le = (logits[:, N_GROUPS:nr] + b_re.astype(F32)).reshape(T, N_GROUPS, EXPERTS_PER_GROUP)
    le = jnp.take_along_axis(le, gsel[:, None, None], axis=1)[:, 0]
    pe = jax.nn.softmax(le, axis=-1)
    top_p, top_i = lax.top_k(pe, TOP_K)
    gate = pg_sel * top_p / jnp.sum(top_p, axis=-1, keepdims=True)
    eid = (gsel[:, None] * EXPERTS_PER_GROUP + top_i.astype(jnp.int32)).reshape(-1)
    gw = gate.reshape(-1)

    na = T * TOP_K
    order = jnp.argsort(eid).astype(jnp.int32)
    counts = jnp.bincount(eid, length=ne).astype(jnp.int32)
    start = jnp.cumsum(counts) - counts
    pcounts = (counts + tm - 1) // tm * tm
    pend = jnp.cumsum(pcounts)
    pstart = pend - pcounts
    n_rows = na + ne * tm
    n_blk = n_rows // tm
    blk_first = jnp.arange(n_blk, dtype=jnp.int32) * tm
    blk_e = jnp.minimum(jnp.sum((pend[None, :] <= blk_first[:, None]).astype(jnp.int32), axis=1), ne - 1)
    rep = lambda per_blk: jnp.repeat(per_blk, tm)
    rows = jnp.arange(n_rows, dtype=jnp.int32)
    off = rows - rep(pstart[blk_e])
    live = off < rep(counts[blk_e])
    row_a = order[jnp.clip(rep(start[blk_e]) + off, 0, na - 1)]
    row_tok = jnp.where(live, row_a // TOP_K, 0)
    row_w = jnp.where(live, gw[row_a], 0.0)
    row_dst = jnp.where(live, (row_a % TOP_K) * T + row_a // TOP_K, TOP_K * T + rows % tm)
    n_used = (pend[-1:] // tm).astype(jnp.int32)

    de = w_gate.shape[-1]
    idx_blk = lambda shift: pl.BlockSpec(
        (None, 1, tm), lambda i, be, nu: (jnp.minimum(i + shift, n_blk - 1), 0, 0), memory_space=pltpu.SMEM)
    w_in_blk = pl.BlockSpec((None, None, d, de), lambda i, be, nu: (layer, be[i], 0, 0))
    row_tok3 = row_tok.reshape(n_blk, 1, tm)
    y = pl.pallas_call(
        functools.partial(_expert_kernel, dump_row=TOP_K * T),
        grid_spec=pltpu.PrefetchScalarGridSpec(
            num_scalar_prefetch=2,
            grid=(n_blk,),
            in_specs=[idx_blk(0), idx_blk(0), idx_blk(1),
                      pl.BlockSpec(memory_space=pl.ANY),
                      w_in_blk, w_in_blk,
                      pl.BlockSpec((None, None, de, d), lambda i, be, nu: (layer, be[i], 0, 0)),
                      pl.BlockSpec((tm, 1), lambda i, be, nu: (i, 0))],
            out_specs=pl.BlockSpec(memory_space=pl.ANY),
            scratch_shapes=[pltpu.VMEM((d, de), BF16), pltpu.VMEM((d, de), BF16),
                            pltpu.VMEM((de, d), BF16), pltpu.VMEM((2, tm, d // 2), jnp.uint32),
                            pltpu.VMEM((2, tm, d), F32),
                            pltpu.SemaphoreType.DMA((2,)), pltpu.SemaphoreType.DMA((2,))]),
        out_shape=jax.ShapeDtypeStruct((TOP_K * T + tm, d), F32),
        compiler_params=pltpu.CompilerParams(
            dimension_semantics=("arbitrary",), vmem_limit_bytes=VMEM_LIMIT),
        name="moe_experts",
    )(blk_e, n_used, row_dst.reshape(n_blk, 1, tm), row_tok3, row_tok3, hp, w_gate, w_up, w_down,
      row_w[:, None])

    assert TOP_K == 2
    ta = _row_tile(T, 512)
    blk = lambda shift: pl.BlockSpec((ta, d), lambda i: (i + shift, 0))
    out = pl.pallas_call(
        _moe_add_kernel,
        grid=(T // ta,),
        in_specs=[blk(0), blk(0), blk(T // ta)],
        out_specs=blk(0),
        out_shape=jax.ShapeDtypeStruct((T, d), F32),
        compiler_params=pltpu.CompilerParams(dimension_semantics=("parallel",)),
        name="moe_combine",
    )(x2, y, y)
    return out.reshape(nb, L, d)


def encoder_trunk(x, norm_mix, norm_ffn, norm_final, mla_p, hy_p, moe_p):
    for i in range(DEPTH):
        j = i // 2
        if i % 2 == 0:
            x = mla_block(x, norm_mix[i], *[p[j] for p in mla_p])
        else:
            x = hyena_block(x, norm_mix[i], *[p[j] for p in hy_p])
        x = moe_block(x, norm_ffn[i], *[p[i] for p in moe_p[:4]], *moe_p[4:], layer=i)
    return rms_norm_pallas(x, norm_final)


def kernel(x_prompt, x_sample, norm_mix, norm_ffn, norm_final, mla_w_in, mla_q_norm, mla_w_q_up, mla_kv_norm, mla_w_kv_up, mla_w_out, hy_w_in, hy_conv_w, hy_conv_b, hy_f_w1, hy_f_b1, hy_f_freq1, hy_f_w2, hy_f_b2, hy_f_freq2, hy_f_w3, hy_f_b3, hy_skip, hy_w_out, moe_w_group, moe_b_group, moe_w_expert, moe_b_expert, moe_w_gate, moe_w_up, moe_w_down):
    mla_p = (mla_w_in, mla_q_norm, mla_w_q_up, mla_kv_norm, mla_w_kv_up, mla_w_out)
    hy_p = (hy_w_in, hy_conv_w, hy_conv_b, hy_f_w1, hy_f_b1, hy_f_freq1, hy_f_w2, hy_f_b2,
            hy_f_freq2, hy_f_w3, hy_f_b3, hy_skip, hy_w_out)
    moe_p = (moe_w_group, moe_b_group, moe_w_expert, moe_b_expert, moe_w_gate, moe_w_up, moe_w_down)
    y_prompt = encoder_trunk(x_prompt, norm_mix, norm_ffn, norm_final, mla_p, hy_p, moe_p)
    y_sample = encoder_trunk(x_sample, norm_mix, norm_ffn, norm_final, mla_p, hy_p, moe_p)
    return (y_prompt, y_sample)
```

```python
import functools
import math
import jax
import jax.numpy as jnp
from jax import lax
from jax.experimental import pallas as pl
from jax.experimental.pallas import tpu as pltpu

D_MODEL = 2048
DEPTH = 2
N_HEADS = 16
Q_RANK = 512
KV_RANK = 512
QK_NOPE = 128
QK_ROPE = 64
V_DIM = 128
ROPE_THETA = 10000.0
ATTN_SCALE = 1.0 / math.sqrt(QK_NOPE + QK_ROPE)
HY_ORDER = 2
HY_EMB = 33
HY_BANDS = (HY_EMB - 1) // 2
HY_HIDDEN = 64
HY_FAST_DECAY = 0.3
HY_SLOW_DECAY = 1.5
HY_TARGET = 1e-2
N_GROUPS = 8
EXPERTS_PER_GROUP = 8
N_EXPERTS = N_GROUPS * EXPERTS_PER_GROUP
TOP_K = 2
D_EXPERT = 512
ROW_BLOCK = 128
NORM_EPS = 1e-6

F32 = jnp.float32
BF16 = jnp.bfloat16
LANES = 128
HEAD_W = 2 * LANES
VMEM_LIMIT = 56 * 1024 * 1024
FFT_GROUP = 16


def _row_tile(n, want):
    t = min(n, want)
    while n % t:
        t //= 2
    return t


def _rms(x, g):
    return x * lax.rsqrt(jnp.mean(x * x, axis=-1, keepdims=True) + NORM_EPS) * g


def _norm_mm_kernel(x_ref, g_ref, w_ref, o_ref, h_scr):
    @pl.when(pl.program_id(1) == 0)
    def _():
        h_scr[...] = _rms(x_ref[...].astype(F32), g_ref[...]).astype(BF16)

    o_ref[...] = jnp.dot(h_scr[...], w_ref[...], preferred_element_type=F32).astype(o_ref.dtype)


def norm_matmul(x, g, w, out_dtype=F32, tm=512, tn=1024):
    m, k = x.shape
    n = w.shape[1]
    tm = _row_tile(m, tm)
    tn = n if n <= 2048 else _row_tile(n, tn)
    return pl.pallas_call(
        _norm_mm_kernel,
        grid=(m // tm, n // tn),
        in_specs=[pl.BlockSpec((tm, k), lambda i, j: (i, 0)),
                  pl.BlockSpec((1, k), lambda i, j: (0, 0)),
                  pl.BlockSpec((k, tn), lambda i, j: (0, j))],
        out_specs=pl.BlockSpec((tm, tn), lambda i, j: (i, j)),
        out_shape=jax.ShapeDtypeStruct((m, n), out_dtype),
        scratch_shapes=[pltpu.VMEM((tm, k), BF16)],
        compiler_params=pltpu.CompilerParams(
            dimension_semantics=("parallel", "arbitrary"), vmem_limit_bytes=VMEM_LIMIT),
        name="norm_matmul",
    )(x, g.reshape(1, k).astype(F32), w)


def _mm_res_kernel(a_ref, w_ref, r_ref, o_ref):
    acc = jnp.dot(a_ref[...].astype(BF16), w_ref[...], preferred_element_type=F32)
    o_ref[...] = (r_ref[...].astype(F32) + acc).astype(o_ref.dtype)


def matmul_residual(a, w, r, tm=512, tn=1024):
    m, k = a.shape
    n = w.shape[1]
    tm = _row_tile(m, tm)
    tn = _row_tile(n, tn)
    return pl.pallas_call(
        _mm_res_kernel,
        grid=(m // tm, n // tn),
        in_specs=[pl.BlockSpec((tm, k), lambda i, j: (i, 0)),
                  pl.BlockSpec((k, tn), lambda i, j: (0, j)),
                  pl.BlockSpec((tm, tn), lambda i, j: (i, j))],
        out_specs=pl.BlockSpec((tm, tn), lambda i, j: (i, j)),
        out_shape=jax.ShapeDtypeStruct((m, n), r.dtype),
        compiler_params=pltpu.CompilerParams(
            dimension_semantics=("parallel", "parallel"), vmem_limit_bytes=VMEM_LIMIT),
        name="matmul_residual",
    )(a, w, r)


def _rms_kernel(x_ref, g_ref, o_ref):
    o_ref[...] = _rms(x_ref[...].astype(F32), g_ref[...]).astype(o_ref.dtype)


def rms_norm_pallas(x, g, rows=512):
    shp = x.shape
    d = shp[-1]
    x2 = x.reshape(-1, d)
    t = x2.shape[0]
    rows = _row_tile(t, rows)
    out = pl.pallas_call(
        _rms_kernel,
        grid=(t // rows,),
        in_specs=[pl.BlockSpec((rows, d), lambda i: (i, 0)),
                  pl.BlockSpec((1, d), lambda i: (0, 0))],
        out_specs=pl.BlockSpec((rows, d), lambda i: (i, 0)),
        out_shape=jax.ShapeDtypeStruct((t, d), x.dtype),
        compiler_params=pltpu.CompilerParams(dimension_semantics=("parallel",)),
        name="rms_norm",
    )(x2, g.reshape(1, d).astype(F32))
    return out.reshape(shp)


def _mla_qkv_kernel(c_ref, qg_ref, kvg_ref, wq_ref, wkv_ref, cq_ref, sq_ref, ck_ref, sk_ref,
                    q_ref, k_ref, v_ref, *, n_heads, q_rank, kv_rank):
    c = c_ref[...]
    cqn = _rms(c[:, :q_rank], qg_ref[...]).astype(BF16)
    ckvn = _rms(c[:, q_rank:q_rank + kv_rank], kvg_ref[...]).astype(BF16)
    kr = c[:, q_rank + kv_rank:]
    k_hi = (kr * ck_ref[...] + pltpu.roll(kr, LANES // 2, axis=1) * sk_ref[...]).astype(BF16)
    tr = c.shape[0]
    lane = lax.broadcasted_iota(jnp.int32, (tr, LANES), 1)
    v_hi = jnp.where(lane == 0, 1.0, 0.0).astype(BF16)
    cq = cq_ref[...]
    sq = sq_ref[...]
    qw = 3 * LANES
    for h in range(n_heads):
        a = jnp.dot(cqn, wq_ref[:, h * qw:(h + 1) * qw], preferred_element_type=F32)
        q_ref[h, :, :LANES] = (a[:, :LANES] * ATTN_SCALE).astype(BF16)
        q_ref[h, :, LANES:] = (a[:, LANES:2 * LANES] * cq + a[:, 2 * LANES:] * sq).astype(BF16)
        kv = jnp.dot(ckvn, wkv_ref[:, h * HEAD_W:(h + 1) * HEAD_W], preferred_element_type=F32)
        k_ref[h, :, :LANES] = kv[:, :LANES].astype(BF16)
        k_ref[h, :, LANES:] = k_hi
        v_ref[h, :, :LANES] = kv[:, LANES:].astype(BF16)
        v_ref[h, :, LANES:] = v_hi


def _attn_kernel(q_ref, k_ref, v_ref, o_ref, *, tk, unroll):
    q = q_ref[...]
    tq = q.shape[0]
    nk = k_ref.shape[0] // tk

    def body(j, carry):
        m, acc = carry
        off = pl.multiple_of(j * tk, tk)
        s = lax.dot_general(q, k_ref[pl.ds(off, tk), :], (((1,), (1,)), ((), ())),
                            preferred_element_type=F32)
        m_new = jnp.maximum(m, jnp.max(s, axis=1, keepdims=True))
        p = jnp.exp((s - m_new).astype(BF16))
        acc = acc * jnp.exp(m - m_new) + jnp.dot(p, v_ref[pl.ds(off, tk), :],
                                                 preferred_element_type=F32)
        return m_new, acc

    m0 = jnp.full((tq, 1), -1e30, F32)
    acc0 = jnp.zeros((tq, HEAD_W), F32)
    _, acc = lax.fori_loop(0, nk, body, (m0, acc0), unroll=min(unroll, nk))
    o_ref[...] = (acc[:, :LANES] / acc[:, LANES:LANES + 1]).astype(o_ref.dtype)


def mla_attention(q, k, v, tq=512, tk=1024, unroll=16):
    b, nh, L, _ = q.shape
    tq = _row_tile(L, tq)
    tk = _row_tile(L, tk)
    kv_spec = pl.BlockSpec((None, None, L, HEAD_W), lambda bi, h, i: (bi, h, 0, 0))
    return pl.pallas_call(
        functools.partial(_attn_kernel, tk=tk, unroll=unroll),
        grid=(b, nh, L // tq),
        in_specs=[pl.BlockSpec((None, None, tq, HEAD_W), lambda bi, h, i: (bi, h, i, 0)),
                  kv_spec, kv_spec],
        out_specs=pl.BlockSpec((None, tq, V_DIM), lambda bi, h, i: (bi, i, h)),
        out_shape=jax.ShapeDtypeStruct((b, L, nh * V_DIM), BF16),
        compiler_params=pltpu.CompilerParams(
            dimension_semantics=("parallel", "parallel", "arbitrary"),
            vmem_limit_bytes=VMEM_LIMIT),
        name="mla_attention",
    )(q, k, v)


def _rot_cols(w):
    half = w.shape[-1] // 2
    return jnp.concatenate([-w[..., half:], w[..., :half]], axis=-1)


def mla_block(x, norm_g, w_in, q_norm, w_q_up, kv_norm, w_kv_up, w_out, tq=512, tk=1024):
    assert QK_NOPE == LANES and V_DIM == LANES and QK_ROPE == LANES // 2
    b, L, d = x.shape
    t = b * L
    nh = N_HEADS
    x2 = x.reshape(t, d)
    rk = Q_RANK + KV_RANK
    w_in_ext = jnp.concatenate([w_in, _rot_cols(w_in[:, rk:])], axis=1).astype(BF16)
    c = norm_matmul(x2, norm_g, w_in_ext)

    wq = w_q_up.reshape(Q_RANK, nh, QK_NOPE + QK_ROPE)
    rope_w = wq[..., QK_NOPE:]
    z = jnp.zeros_like(rope_w)
    wq3 = jnp.concatenate([wq[..., :QK_NOPE], rope_w, z, _rot_cols(rope_w), z], axis=-1)
    wq3 = wq3.reshape(Q_RANK, nh * 3 * LANES).astype(BF16)
    wkv = w_kv_up.astype(BF16)

    inv = 1.0 / (ROPE_THETA ** (jnp.arange(0, QK_ROPE, 2, dtype=F32) / QK_ROPE))
    ang = jnp.arange(L, dtype=F32)[:, None] * inv[None, :]
    ang = jnp.concatenate([ang, ang], axis=-1)
    zpad = jnp.zeros((L, LANES - QK_ROPE), F32)
    cos_k = jnp.concatenate([jnp.cos(ang), zpad], axis=1)
    sin_k = jnp.concatenate([jnp.sin(ang), zpad], axis=1)
    cos_q = cos_k * ATTN_SCALE
    sin_q = sin_k * ATTN_SCALE

    tr = _row_tile(L, 256)
    nl = L // tr
    cw = c.shape[1]
    tab = pl.BlockSpec((tr, LANES), lambda i: (i % nl, 0))
    slab = pl.BlockSpec((None, nh, tr, HEAD_W), lambda i: (i // nl, 0, i % nl, 0))
    slab_shape = jax.ShapeDtypeStruct((b, nh, L, HEAD_W), BF16)
    q, k, v = pl.pallas_call(
        functools.partial(_mla_qkv_kernel, n_heads=nh, q_rank=Q_RANK, kv_rank=KV_RANK),
        grid=(t // tr,),
        in_specs=[pl.BlockSpec((tr, cw), lambda i: (i, 0)),
                  pl.BlockSpec((1, Q_RANK), lambda i: (0, 0)),
                  pl.BlockSpec((1, KV_RANK), lambda i: (0, 0)),
                  pl.BlockSpec(wq3.shape, lambda i: (0, 0)),
                  pl.BlockSpec(wkv.shape, lambda i: (0, 0)),
                  tab, tab, tab, tab],
        out_specs=[slab, slab, slab],
        out_shape=[slab_shape, slab_shape, slab_shape],
        compiler_params=pltpu.CompilerParams(
            dimension_semantics=("parallel",), vmem_limit_bytes=VMEM_LIMIT),
        name="mla_qkv",
    )(c, q_norm.reshape(1, -1).astype(F32), kv_norm.reshape(1, -1).astype(F32), wq3, wkv,
      cos_q, sin_q, cos_k, sin_k)

    o = mla_attention(q, k, v, tq=tq, tk=tk)

    out = matmul_residual(o.reshape(t, nh * V_DIM), w_out.astype(BF16), x2)
    return out.reshape(b, L, d)


def _fft_split(L):
    n = 2 * L
    n2 = min(256, n // 16)
    return n // n2, n2


def _fft_tables(L):
    n1, n2 = _fft_split(L)
    n = n1 * n2
    n1h = n1 // 2
    k1n = n1 // 2 + 1
    k1p = min(n1, -(-k1n // 8) * 8)
    g = FFT_GROUP
    nb = n2 // g
    k1 = jnp.arange(k1p, dtype=jnp.int32)
    m1 = jnp.arange(n1h, dtype=jnp.int32)
    m2 = jnp.arange(n2, dtype=jnp.int32)
    live = (k1 < k1n).astype(F32)
    wgt = jnp.where((k1 == 0) | (k1 == n1 // 2), 1.0, 2.0) * live / n1
    th1 = ((k1[:, None] * m1[None, :]) % n1).astype(F32) * (2.0 * math.pi / n1)
    c1, s1 = jnp.cos(th1), jnp.sin(th1)
    eye = jnp.eye(g, dtype=F32)
    fa = jnp.concatenate([jnp.kron(c1 * live[:, None], eye), jnp.kron(-s1 * live[:, None], eye)],
                         axis=0).astype(BF16)
    gc = jnp.concatenate([jnp.kron((c1 * wgt[:, None]).T, eye), jnp.kron((-s1 * wgt[:, None]).T, eye)],
                         axis=1).astype(BF16)
    tht = ((k1[:, None] * m2[None, :]) % n).astype(F32) * (2.0 * math.pi / n)
    tw = jnp.stack([jnp.cos(tht), -jnp.sin(tht)], axis=0).reshape(2, k1p, nb, g)
    tw = tw.transpose(0, 2, 1, 3).reshape(2, nb, k1p * g, 1)
    idx2 = (m2[:, None] * m2[None, :]) % n2
    th2 = idx2.astype(F32) * (2.0 * math.pi / n2)
    c2, s2 = jnp.cos(th2), jnp.sin(th2)
    mf = jnp.concatenate([jnp.concatenate([c2, s2], 1), jnp.concatenate([-s2, c2], 1)], 0).astype(BF16)
    mi = (jnp.concatenate([jnp.concatenate([c2, -s2], 1), jnp.concatenate([s2, c2], 1)], 0) / n2).astype(BF16)
    return fa, gc, tw, mf, mi


def _fft_a_kernel(x_ref, fa_ref, tw_ref, y_ref):
    n1h, g, tc = x_ref.shape
    y = jnp.dot(fa_ref[...], x_ref[...].reshape(n1h * g, tc), preferred_element_type=F32)
    half = y.shape[0] // 2
    yr, yi = y[:half], y[half:]
    tr, ti = tw_ref[0], tw_ref[1]
    y_ref[0] = (yr * tr - yi * ti).reshape(y_ref.shape[1:]).astype(y_ref.dtype)
    y_ref[1] = (yr * ti + yi * tr).reshape(y_ref.shape[1:]).astype(y_ref.dtype)


def fft_stage_a(x, fa, tw, tc=512):
    s, L, d = x.shape
    rows, cols = fa.shape
    nb = tw.shape[1]
    g = FFT_GROUP
    n1h, k1p, n2 = cols // g, rows // (2 * g), nb * g
    tc = _row_tile(d, tc)
    return pl.pallas_call(
        _fft_a_kernel,
        grid=(s, nb, d // tc),
        in_specs=[pl.BlockSpec((None, n1h, g, tc), lambda b, j, c: (b, 0, j, c)),
                  pl.BlockSpec((rows, cols), lambda b, j, c: (0, 0)),
                  pl.BlockSpec((2, None, k1p * g, 1), lambda b, j, c: (0, j, 0, 0))],
        out_specs=pl.BlockSpec((None, 2, k1p, g, tc), lambda b, j, c: (b, 0, 0, j, c)),
        out_shape=jax.ShapeDtypeStruct((s, 2, k1p, n2, d), BF16),
        compiler_params=pltpu.CompilerParams(
            dimension_semantics=("parallel", "parallel", "parallel"), vmem_limit_bytes=VMEM_LIMIT),
        name="fft_stage_a",
    )(x.reshape(s, n1h, n2, d), fa, tw)


def _fft_b_kernel(y_ref, h_ref, mf_ref, mi_ref, u_ref):
    n2 = y_ref.shape[1]
    tc = y_ref.shape[2]
    z = jnp.dot(mf_ref[...], y_ref[...].reshape(2 * n2, tc), preferred_element_type=F32)
    zr, zi = z[:n2], z[n2:]
    hr, hi = h_ref[0], h_ref[1]
    w = jnp.concatenate([zr * hr - zi * hi, zr * hi + zi * hr], axis=0).astype(BF16)
    u = jnp.dot(mi_ref[...], w, preferred_element_type=F32)
    u_ref[0] = u[:n2].astype(u_ref.dtype)
    u_ref[1] = u[n2:].astype(u_ref.dtype)


def fft_stage_b(y, spec, order, mf, mi, tc=1024):
    s, _, k1p, n2, d = y.shape
    tc = _row_tile(d, tc)
    blk = pl.BlockSpec((None, 2, None, n2, tc), lambda k, j, b: (b, 0, k, 0, j))
    return pl.pallas_call(
        _fft_b_kernel,
        grid=(k1p, d // tc, s),
        in_specs=[blk,
                  pl.BlockSpec((None, 2, None, n2, tc), lambda k, j, b: (order, 0, k, 0, j)),
                  pl.BlockSpec(mf.shape, lambda k, j, b: (0, 0)),
                  pl.BlockSpec(mi.shape, lambda k, j, b: (0, 0))],
        out_specs=blk,
        out_shape=jax.ShapeDtypeStruct(y.shape, BF16),
        compiler_params=pltpu.CompilerParams(
            dimension_semantics=("parallel", "parallel", "arbitrary"), vmem_limit_bytes=VMEM_LIMIT),
        name="fft_stage_b",
    )(y, spec, mf, mi)


def _fft_c_kernel(u_ref, gc_ref, tw_ref, v_ref, g_ref, sk_ref, o_ref):
    _, k1p, g, tc = u_ref.shape
    rows = v_ref.shape[0] * g
    ur = u_ref[0].reshape(k1p * g, tc).astype(F32)
    ui = u_ref[1].reshape(k1p * g, tc).astype(F32)
    tr, ti = tw_ref[0], tw_ref[1]
    w = jnp.concatenate([ur * tr + ui * ti, ui * tr - ur * ti], axis=0).astype(BF16)
    y = jnp.dot(gc_ref[...], w, preferred_element_type=F32)
    v = v_ref[...].reshape(rows, tc).astype(F32)
    gate = g_ref[...].reshape(rows, tc).astype(F32)
    o_ref[...] = (gate * (y + v * sk_ref[...])).astype(o_ref.dtype).reshape(o_ref.shape)


def fft_stage_c(u, gc, tw, v, gate, skip, tc=512):
    s, _, k1p, n2, d = u.shape
    rows, cols = gc.shape
    g = FFT_GROUP
    nb = n2 // g
    n1h = rows // g
    L = n1h * n2
    tc = _row_tile(d, tc)
    row = pl.BlockSpec((None, n1h, g, tc), lambda b, j, c: (b, 0, j, c))
    out = pl.pallas_call(
        _fft_c_kernel,
        grid=(s, nb, d // tc),
        in_specs=[pl.BlockSpec((None, 2, k1p, g, tc), lambda b, j, c: (b, 0, 0, j, c)),
                  pl.BlockSpec((rows, cols), lambda b, j, c: (0, 0)),
                  pl.BlockSpec((2, None, k1p * g, 1), lambda b, j, c: (0, j, 0, 0)),
                  row, row,
                  pl.BlockSpec((1, tc), lambda b, j, c: (0, c))],
        out_specs=row,
        out_shape=jax.ShapeDtypeStruct((s, n1h, n2, d), BF16),
        compiler_params=pltpu.CompilerParams(
            dimension_semantics=("parallel", "parallel", "parallel"), vmem_limit_bytes=VMEM_LIMIT),
        name="fft_stage_c",
    )(u, gc, tw, v.reshape(s, n1h, n2, d), gate.reshape(s, n1h, n2, d), skip.reshape(1, d).astype(F32))
    return out.reshape(s, L, d)


def _filter_kernel(ft_ref, w1_ref, b1_ref, f1_ref, w2_ref, b2_ref, f2_ref, w3_ref, b3_ref, dl_ref,
                   taps_ref, sum_ref, *, seq_len):
    i = pl.program_id(1)
    tl = ft_ref.shape[0]
    a = jnp.sin(f1_ref[...] * (jnp.dot(ft_ref[...], w1_ref[...], preferred_element_type=F32) + b1_ref[...]))
    a = jnp.sin(f2_ref[...] * (jnp.dot(a, w2_ref[...], preferred_element_type=F32) + b2_ref[...]))
    f = jnp.dot(a, w3_ref[...], preferred_element_type=F32) + b3_ref[...]
    pos = (lax.broadcasted_iota(jnp.int32, (tl, 1), 0) + i * tl).astype(F32)
    t = pos * (1.0 / max(seq_len - 1, 1))
    f = f * jnp.exp(-t * dl_ref[...])

    @pl.when(i == 0)
    def _():
        sum_ref[...] = jnp.zeros_like(sum_ref)

    sum_ref[...] += jnp.sum(jnp.abs(f), axis=0, keepdims=True)
    is_bwd = pl.program_id(0) % 2 == 1
    f = jnp.where(jnp.logical_and(is_bwd, pos == 0.0), 0.0, f)
    taps_ref[...] = f.astype(taps_ref.dtype)


def hyena_filter_taps(L, d, f_w1, f_b1, f_fr1, f_w2, f_b2, f_fr2, f_w3, f_b3, tl=512):
    hp = LANES
    t = jnp.linspace(0.0, 1.0, L, dtype=F32)[:, None]
    w = (2.0 * math.pi / L) * jnp.arange(L, dtype=F32)[:, None]
    bands = jnp.linspace(1e-4, HY_BANDS - 1, HY_BANDS, dtype=F32)[None, :]
    feats = jnp.concatenate([t, jnp.cos(bands * w), -jnp.sin(bands * w),
                             jnp.zeros((L, hp - HY_EMB), F32)], axis=-1)

    def pad2(m, r, c):
        return jnp.pad(m.astype(F32), ((0, r - m.shape[0]), (0, c - m.shape[1])))

    w1 = pad2(f_w1, hp, hp)
    w2 = pad2(f_w2, hp, hp)
    w3 = pad2(f_w3, hp, f_w3.shape[1])
    b1, fr1 = pad2(f_b1[None], 1, hp), pad2(f_fr1[None], 1, hp)
    b2, fr2 = pad2(f_b2[None], 1, hp), pad2(f_fr2[None], 1, hp)
    max_decay = math.log(HY_TARGET) / HY_FAST_DECAY
    min_decay = math.log(HY_TARGET) / HY_SLOW_DECAY
    deltas = jnp.abs(jnp.linspace(min_decay, max_decay, d, dtype=F32))[None, :]
    ns = HY_ORDER * 2
    tl = _row_tile(L, tl)
    vec = lambda n: pl.BlockSpec((1, n), lambda s, i: (0, 0))
    mat = lambda r, c: pl.BlockSpec((r, c), lambda s, i: (0, 0))
    return pl.pallas_call(
        functools.partial(_filter_kernel, seq_len=L),
        grid=(ns, L // tl),
        in_specs=[pl.BlockSpec((tl, hp), lambda s, i: (i, 0)),
                  mat(hp, hp), vec(hp), vec(hp), mat(hp, hp), vec(hp), vec(hp),
                  pl.BlockSpec((hp, d), lambda s, i: (0, s)),
                  pl.BlockSpec((1, d), lambda s, i: (0, s)),
                  vec(d)],
        out_specs=[pl.BlockSpec((None, tl, d), lambda s, i: (s, i, 0)),
                   pl.BlockSpec((None, 1, d), lambda s, i: (s, 0, 0))],
        out_shape=[jax.ShapeDtypeStruct((ns, L, d), BF16), jax.ShapeDtypeStruct((ns, 1, d), F32)],
        compiler_params=pltpu.CompilerParams(dimension_semantics=("parallel", "arbitrary")),
        name="hyena_filter_taps",
    )(feats, w1, b1, fr1, w2, b2, fr2, w3, f_b3[None].astype(F32), deltas)


def _fft_bf_kernel(yf_ref, yb_ref, sf_ref, sb_ref, mf_ref, h_ref):
    n2 = yf_ref.shape[1]
    tc = yf_ref.shape[2]
    zf = jnp.dot(mf_ref[...], yf_ref[...].reshape(2 * n2, tc), preferred_element_type=F32)
    zb = jnp.dot(mf_ref[...], yb_ref[...].reshape(2 * n2, tc), preferred_element_type=F32)
    sf = 1.0 / (sf_ref[...] + 1e-6)
    sb = 1.0 / (sb_ref[...] + 1e-6)
    h_ref[0] = zf[:n2] * sf + zb[:n2] * sb
    h_ref[1] = zf[n2:] * sf - zb[n2:] * sb


def hyena_spectrum(L, d, fa, tw, mf, *filter_params, tc=1024):
    taps, sums = hyena_filter_taps(L, d, *filter_params)
    yv = fft_stage_a(taps, fa, tw)
    _, _, n1, n2, _ = yv.shape
    tc = _row_tile(d, tc)
    side = lambda sd: pl.BlockSpec((None, 2, None, n2, tc), lambda o, k, j: (2 * o + sd, 0, k, 0, j))
    norm = lambda sd: pl.BlockSpec((None, 1, tc), lambda o, k, j: (2 * o + sd, 0, j))
    return pl.pallas_call(
        _fft_bf_kernel,
        grid=(HY_ORDER, n1, d // tc),
        in_specs=[side(0), side(1), norm(0), norm(1), pl.BlockSpec(mf.shape, lambda o, k, j: (0, 0))],
        out_specs=pl.BlockSpec((None, 2, None, n2, tc), lambda o, k, j: (o, 0, k, 0, j)),
        out_shape=jax.ShapeDtypeStruct((HY_ORDER, 2, n1, n2, d), F32),
        compiler_params=pltpu.CompilerParams(
            dimension_semantics=("parallel", "parallel", "parallel"), vmem_limit_bytes=VMEM_LIMIT),
        name="fft_stage_b_filter",
    )(yv, yv, sums, sums, mf)


def _shortconv_kernel(u_ref, up_ref, un_ref, w_ref, b_ref, v_ref, g1_ref, g2_ref):
    i = pl.program_id(1)
    tr = u_ref.shape[0]
    d = v_ref.shape[1]
    hr = up_ref.shape[0]
    first = i == 0
    last = i == pl.num_programs(1) - 1
    for j, o_ref in enumerate((v_ref, g1_ref, g2_ref)):
        cols = slice(j * d, (j + 1) * d)
        u = u_ref[:, cols].astype(F32)
        row = lax.broadcasted_iota(jnp.int32, u.shape, 0)
        prev_row = jnp.where(first, 0.0, up_ref[hr - 1:hr, cols].astype(F32))
        next_row = jnp.where(last, 0.0, un_ref[0:1, cols].astype(F32))
        um = jnp.where(row == 0, prev_row, pltpu.roll(u, 1, axis=0))
        up = jnp.where(row == tr - 1, next_row, pltpu.roll(u, tr - 1, axis=0))
        w = w_ref[:, cols]
        o_ref[...] = (um * w[0:1] + u * w[1:2] + up * w[2:3] + b_ref[:, cols]).astype(o_ref.dtype)


def hyena_shortconv(u, conv_w, conv_b, tr=256):
    b, L, d3 = u.shape
    d = d3 // 3
    tr = _row_tile(L, tr)
    hr = 16
    hb = tr // hr
    nhb = L // hr
    out = pl.BlockSpec((None, tr, d), lambda bi, i: (bi, i, 0))
    shape = jax.ShapeDtypeStruct((b, L, d), BF16)
    return pl.pallas_call(
        _shortconv_kernel,
        grid=(b, L // tr),
        in_specs=[pl.BlockSpec((None, tr, d3), lambda bi, i: (bi, i, 0)),
                  pl.BlockSpec((None, hr, d3), lambda bi, i: (bi, jnp.maximum(i * hb - 1, 0), 0)),
                  pl.BlockSpec((None, hr, d3), lambda bi, i: (bi, jnp.minimum((i + 1) * hb, nhb - 1), 0)),
                  pl.BlockSpec((3, d3), lambda bi, i: (0, 0)),
                  pl.BlockSpec((1, d3), lambda bi, i: (0, 0))],
        out_specs=[out, out, out],
        out_shape=[shape, shape, shape],
        compiler_params=pltpu.CompilerParams(dimension_semantics=("parallel", "parallel")),
        name="hyena_shortconv",
    )(u, u, u, conv_w.astype(F32), conv_b[None].astype(F32))


def hyena_block(x, norm_g, w_in, conv_w, conv_b, f_w1, f_b1, f_fr1, f_w2, f_b2, f_fr2, f_w3, f_b3,
                skip, w_out):
    b, L, d = x.shape
    t = b * L
    x2 = x.reshape(t, d)
    u = norm_matmul(x2, norm_g, w_in.astype(BF16), out_dtype=BF16).reshape(b, L, 3 * d)
    v, g1, g2 = hyena_shortconv(u, conv_w, conv_b)
    fa, gc, tw, mf, mi = _fft_tables(L)
    spec = hyena_spectrum(L, d, fa, tw, mf, f_w1, f_b1, f_fr1, f_w2, f_b2, f_fr2, f_w3, f_b3)
    z = fft_stage_c(fft_stage_b(fft_stage_a(v, fa, tw), spec, 0, mf, mi), gc, tw, v, g1, skip[0])
    y = fft_stage_c(fft_stage_b(fft_stage_a(z, fa, tw), spec, 1, mf, mi), gc, tw, z, g2, skip[1])
    out = matmul_residual(y.reshape(t, d), w_out.astype(BF16), x2)
    return out.reshape(b, L, d)


def _router_kernel(x_ref, g_ref, w_ref, lg_ref, hp_ref):
    h = _rms(x_ref[...].astype(F32), g_ref[...]).astype(BF16)
    lg_ref[...] = jnp.dot(h, w_ref[...], preferred_element_type=F32)
    bits = lax.bitcast_convert_type(h.astype(F32), jnp.uint32)
    half = bits.shape[1] // 2
    hp_ref[...] = bits[:, :half] | (bits[:, half:] >> 16)


def _unpack_rows(packed):
    hi = lax.bitcast_convert_type(packed & jnp.uint32(0xFFFF0000), F32)
    lo = lax.bitcast_convert_type(packed << 16, F32)
    return jnp.concatenate([hi, lo], axis=1).astype(BF16)


def _expert_kernel(be_ref, nu_ref, dst_ref, tok_ref, tok_next_ref, hp_hbm, wg_ref, wu_ref, wd_ref, rw_ref,
                   o_hbm, wg_s, wu_s, wd_s, x_s, y_s, gsem, sem, *, dump_row):
    i = pl.program_id(0)
    tm = rw_ref.shape[0]
    used = i < nu_ref[0]
    slot = i % 2

    def block_rows(s):
        return pltpu.make_async_copy(y_s.at[s], o_hbm.at[pl.ds(dump_row, tm), :], sem.at[s])

    def gathered_rows(s):
        return pltpu.make_async_copy(hp_hbm.at[pl.ds(0, tm), :], x_s.at[s], gsem.at[s])

    def start_gather(idx_ref, s):
        for r in range(tm):
            pltpu.make_async_copy(hp_hbm.at[pl.ds(idx_ref[0, r], 1), :],
                                  x_s.at[s, pl.ds(r, 1), :], gsem.at[s]).start()

    @pl.when(jnp.logical_and(i == 0, used))
    def _():
        start_gather(tok_ref, 0)

    @pl.when(i == nu_ref[0])
    def _():
        gathered_rows(slot).wait()

    @pl.when(i == 0)
    def _():
        y_s[1] = jnp.zeros(y_s.shape[1:], y_s.dtype)
        block_rows(1).start()
        block_rows(1).wait()

    @pl.when(jnp.logical_or(i == 0, be_ref[i] != be_ref[jnp.maximum(i - 1, 0)]))
    def _():
        wg_s[...] = wg_ref[...].astype(BF16)
        wu_s[...] = wu_ref[...].astype(BF16)
        wd_s[...] = wd_ref[...].astype(BF16)

    @pl.when(used)
    def _():
        start_gather(tok_next_ref, 1 - slot)
        gathered_rows(slot).wait()
        x = _unpack_rows(x_s[slot])
        g = jnp.dot(x, wg_s[...], preferred_element_type=F32)
        u = jnp.dot(x, wu_s[...], preferred_element_type=F32)
        a = (g * jax.nn.sigmoid(g) * u).astype(BF16)
        y = jnp.dot(a, wd_s[...], preferred_element_type=F32)
        y_s[slot] = y * rw_ref[...]

    @pl.when(jnp.logical_and(i >= 1, i - 1 < nu_ref[0]))
    def _():
        block_rows(1 - slot).wait()

    @pl.when(used)
    def _():
        for r in range(tm):
            pltpu.make_async_copy(y_s.at[slot, pl.ds(r, 1), :],
                                  o_hbm.at[pl.ds(dst_ref[0, r], 1), :], sem.at[slot]).start()

    @pl.when(jnp.logical_and(i == pl.num_programs(0) - 1, used))
    def _():
        block_rows(slot).wait()


def _moe_add_kernel(x_ref, a_ref, b_ref, o_ref):
    o_ref[...] = x_ref[...] + a_ref[...] + b_ref[...]


def moe_block(x, norm_g, w_rg, b_rg, w_re, b_re, w_gate, w_up, w_down, layer, tm=256):
    nb, L, d = x.shape
    T = nb * L
    ne = N_EXPERTS
    x2 = x.reshape(T, d)
    nr = N_GROUPS + ne
    w_r = jnp.concatenate([w_rg, w_re, jnp.zeros((d, LANES - nr), w_rg.dtype)], axis=1).astype(BF16)
    tr = _row_tile(T, 512)
    logits, hp = pl.pallas_call(
        _router_kernel,
        grid=(T // tr,),
        in_specs=[pl.BlockSpec((tr, d), lambda i: (i, 0)),
                  pl.BlockSpec((1, d), lambda i: (0, 0)),
                  pl.BlockSpec((d, LANES), lambda i: (0, 0))],
        out_specs=[pl.BlockSpec((tr, LANES), lambda i: (i, 0)),
                   pl.BlockSpec((tr, d // 2), lambda i: (i, 0))],
        out_shape=[jax.ShapeDtypeStruct((T, LANES), F32), jax.ShapeDtypeStruct((T, d // 2), jnp.uint32)],
        compiler_params=pltpu.CompilerParams(dimension_semantics=("parallel",)),
        name="moe_router",
    )(x2, norm_g.reshape(1, d).astype(F32), w_r)

    lg = logits[:, :N_GROUPS] + b_rg.astype(F32)
    pg = jax.nn.softmax(lg, axis=-1)
    gsel = jnp.argmax(lg, axis=-1).astype(jnp.int32)
    pg_sel = jnp.take_along_axis(pg, gsel[:, None], axis=1)
    le = (logits[:, N_GROUPS:nr] + b_re.astype(F32)).reshape(T, N_GROUPS, EXPERTS_PER_GROUP)
    le = jnp.take_along_axis(le, gsel[:, None, None], axis=1)[:, 0]
    pe = jax.nn.softmax(le, axis=-1)
    top_p, top_i = lax.top_k(pe, TOP_K)
    gate = pg_sel * top_p / jnp.sum(top_p, axis=-1, keepdims=True)
    eid = (gsel[:, None] * EXPERTS_PER_GROUP + top_i.astype(jnp.int32)).reshape(-1)
    gw = gate.reshape(-1)

    na = T * TOP_K
    order = jnp.argsort(eid).astype(jnp.int32)
    counts = jnp.bincount(eid, length=ne).astype(jnp.int32)
    start = jnp.cumsum(counts) - counts
    pcounts = (counts + tm - 1) // tm * tm
    pend = jnp.cumsum(pcounts)
    pstart = pend - pcounts
    n_rows = na + ne * tm
    n_blk = n_rows // tm
    blk_first = jnp.arange(n_blk, dtype=jnp.int32) * tm
    blk_e = jnp.minimum(jnp.sum((pend[None, :] <= blk_first[:, None]).astype(jnp.int32), axis=1), ne - 1)
    rep = lambda per_blk: jnp.repeat(per_blk, tm)
    rows = jnp.arange(n_rows, dtype=jnp.int32)
    off = rows - rep(pstart[blk_e])
    live = off < rep(counts[blk_e])
    row_a = order[jnp.clip(rep(start[blk_e]) + off, 0, na - 1)]
    row_tok = jnp.where(live, row_a // TOP_K, 0)
    row_w = jnp.where(live, gw[row_a], 0.0)
    row_dst = jnp.where(live, (row_a % TOP_K) * T + row_a // TOP_K, TOP_K * T + rows % tm)
    n_used = (pend[-1:] // tm).astype(jnp.int32)

    de = w_gate.shape[-1]
    idx_blk = lambda shift: pl.BlockSpec(
        (None, 1, tm), lambda i, be, nu: (jnp.minimum(i + shift, n_blk - 1), 0, 0), memory_space=pltpu.SMEM)
    w_in_blk = pl.BlockSpec((None, None, d, de), lambda i, be, nu: (layer, be[i], 0, 0))
    row_tok3 = row_tok.reshape(n_blk, 1, tm)
    y = pl.pallas_call(
        functools.partial(_expert_kernel, dump_row=TOP_K * T),
        grid_spec=pltpu.PrefetchScalarGridSpec(
            num_scalar_prefetch=2,
            grid=(n_blk,),
            in_specs=[idx_blk(0), idx_blk(0), idx_blk(1),
                      pl.BlockSpec(memory_space=pl.ANY),
                      w_in_blk, w_in_blk,
                      pl.BlockSpec((None, None, de, d), lambda i, be, nu: (layer, be[i], 0, 0)),
                      pl.BlockSpec((tm, 1), lambda i, be, nu: (i, 0))],
            out_specs=pl.BlockSpec(memory_space=pl.ANY),
            scratch_shapes=[pltpu.VMEM((d, de), BF16), pltpu.VMEM((d, de), BF16),
                            pltpu.VMEM((de, d), BF16), pltpu.VMEM((2, tm, d // 2), jnp.uint32),
                            pltpu.VMEM((2, tm, d), F32),
                            pltpu.SemaphoreType.DMA((2,)), pltpu.SemaphoreType.DMA((2,))]),
        out_shape=jax.ShapeDtypeStruct((TOP_K * T + tm, d), F32),
        compiler_params=pltpu.CompilerParams(
            dimension_semantics=("arbitrary",), vmem_limit_bytes=VMEM_LIMIT),
        name="moe_experts",
    )(blk_e, n_used, row_dst.reshape(n_blk, 1, tm), row_tok3, row_tok3, hp, w_gate, w_up, w_down,
      row_w[:, None])

    assert TOP_K == 2
    ta = _row_tile(T, 512)
    blk = lambda shift: pl.BlockSpec((ta, d), lambda i: (i + shift, 0))
    out = pl.pallas_call(
        _moe_add_kernel,
        grid=(T // ta,),
        in_specs=[blk(0), blk(0), blk(T // ta)],
        out_specs=blk(0),
        out_shape=jax.ShapeDtypeStruct((T, d), F32),
        compiler_params=pltpu.CompilerParams(dimension_semantics=("parallel",)),
        name="moe_combine",
    )(x2, y, y)
    return out.reshape(nb, L, d)


def encoder_trunk(x, norm_mix, norm_ffn, norm_final, mla_p, hy_p, moe_p):
    for i in range(DEPTH):
        j = i // 2
        if i % 2 == 0:
            x = mla_block(x, norm_mix[i], *[p[j] for p in mla_p])
        else:
            x = hyena_block(x, norm_mix[i], *[p[j] for p in hy_p])
        x = moe_block(x, norm_ffn[i], *[p[i] for p in moe_p[:4]], *moe_p[4:], layer=i)
    return rms_norm_pallas(x, norm_final)


def kernel(x_prompt, x_sample, norm_mix, norm_ffn, norm_final, mla_w_in, mla_q_norm, mla_w_q_up, mla_kv_norm, mla_w_kv_up, mla_w_out, hy_w_in, hy_conv_w, hy_conv_b, hy_f_w1, hy_f_b1, hy_f_freq1, hy_f_w2, hy_f_b2, hy_f_freq2, hy_f_w3, hy_f_b3, hy_skip, hy_w_out, moe_w_group, moe_b_group, moe_w_expert, moe_b_expert, moe_w_gate, moe_w_up, moe_w_down):
    mla_p = (mla_w_in, mla_q_norm, mla_w_q_up, mla_kv_norm, mla_w_kv_up, mla_w_out)
    hy_p = (hy_w_in, hy_conv_w, hy_conv_b, hy_f_w1, hy_f_b1, hy_f_freq1, hy_f_w2, hy_f_b2,
            hy_f_freq2, hy_f_w3, hy_f_b3, hy_skip, hy_w_out)
    moe_p = (moe_w_group, moe_b_group, moe_w_expert, moe_b_expert, moe_w_gate, moe_w_up, moe_w_down)
    y_prompt = encoder_trunk(x_prompt, norm_mix, norm_ffn, norm_final, mla_p, hy_p, moe_p)
    y_sample = encoder_trunk(x_sample, norm_mix, norm_ffn, norm_final, mla_p, hy_p, moe_p)
    return (y_prompt, y_sample)
```

```python
import functools
import math
import jax
import jax.numpy as jnp
from jax import lax
from jax.experimental import pallas as pl
from jax.experimental.pallas import tpu as pltpu

D_MODEL = 2048
DEPTH = 2
N_HEADS = 16
Q_RANK = 512
KV_RANK = 512
QK_NOPE = 128
QK_ROPE = 64
V_DIM = 128
ROPE_THETA = 10000.0
ATTN_SCALE = 1.0 / math.sqrt(QK_NOPE + QK_ROPE)
HY_ORDER = 2
HY_EMB = 33
HY_BANDS = (HY_EMB - 1) // 2
HY_HIDDEN = 64
HY_FAST_DECAY = 0.3
HY_SLOW_DECAY = 1.5
HY_TARGET = 1e-2
N_GROUPS = 8
EXPERTS_PER_GROUP = 8
N_EXPERTS = N_GROUPS * EXPERTS_PER_GROUP
TOP_K = 2
D_EXPERT = 512
ROW_BLOCK = 128
NORM_EPS = 1e-6

F32 = jnp.float32
BF16 = jnp.bfloat16
LANES = 128
HEAD_W = 2 * LANES
VMEM_LIMIT = 56 * 1024 * 1024
FFT_GROUP = 16


def _row_tile(n, want):
    t = min(n, want)
    while n % t:
        t //= 2
    return t


def _rms(x, g):
    return x * lax.rsqrt(jnp.mean(x * x, axis=-1, keepdims=True) + NORM_EPS) * g


def _norm_mm_kernel(x_ref, g_ref, w_ref, o_ref, h_scr):
    @pl.when(pl.program_id(1) == 0)
    def _():
        h_scr[...] = _rms(x_ref[...].astype(F32), g_ref[...]).astype(BF16)

    o_ref[...] = jnp.dot(h_scr[...], w_ref[...], preferred_element_type=F32).astype(o_ref.dtype)


def norm_matmul(x, g, w, out_dtype=F32, tm=512, tn=1024):
    m, k = x.shape
    n = w.shape[1]
    tm = _row_tile(m, tm)
    tn = n if n <= 2048 else _row_tile(n, tn)
    return pl.pallas_call(
        _norm_mm_kernel,
        grid=(m // tm, n // tn),
        in_specs=[pl.BlockSpec((tm, k), lambda i, j: (i, 0)),
                  pl.BlockSpec((1, k), lambda i, j: (0, 0)),
                  pl.BlockSpec((k, tn), lambda i, j: (0, j))],
        out_specs=pl.BlockSpec((tm, tn), lambda i, j: (i, j)),
        out_shape=jax.ShapeDtypeStruct((m, n), out_dtype),
        scratch_shapes=[pltpu.VMEM((tm, k), BF16)],
        compiler_params=pltpu.CompilerParams(
            dimension_semantics=("parallel", "arbitrary"), vmem_limit_bytes=VMEM_LIMIT),
        name="norm_matmul",
    )(x, g.reshape(1, k).astype(F32), w)


def _mm_res_kernel(a_ref, w_ref, r_ref, o_ref):
    acc = jnp.dot(a_ref[...].astype(BF16), w_ref[...], preferred_element_type=F32)
    o_ref[...] = (r_ref[...].astype(F32) + acc).astype(o_ref.dtype)


def matmul_residual(a, w, r, tm=512, tn=1024):
    m, k = a.shape
    n = w.shape[1]
    tm = _row_tile(m, tm)
    tn = _row_tile(n, tn)
    return pl.pallas_call(
        _mm_res_kernel,
        grid=(m // tm, n // tn),
        in_specs=[pl.BlockSpec((tm, k), lambda i, j: (i, 0)),
                  pl.BlockSpec((k, tn), lambda i, j: (0, j)),
                  pl.BlockSpec((tm, tn), lambda i, j: (i, j))],
        out_specs=pl.BlockSpec((tm, tn), lambda i, j: (i, j)),
        out_shape=jax.ShapeDtypeStruct((m, n), r.dtype),
        compiler_params=pltpu.CompilerParams(
            dimension_semantics=("parallel", "parallel"), vmem_limit_bytes=VMEM_LIMIT),
        name="matmul_residual",
    )(a, w, r)


def _rms_kernel(x_ref, g_ref, o_ref):
    o_ref[...] = _rms(x_ref[...].astype(F32), g_ref[...]).astype(o_ref.dtype)


def rms_norm_pallas(x, g, rows=512):
    shp = x.shape
    d = shp[-1]
    x2 = x.reshape(-1, d)
    t = x2.shape[0]
    rows = _row_tile(t, rows)
    out = pl.pallas_call(
        _rms_kernel,
        grid=(t // rows,),
        in_specs=[pl.BlockSpec((rows, d), lambda i: (i, 0)),
                  pl.BlockSpec((1, d), lambda i: (0, 0))],
        out_specs=pl.BlockSpec((rows, d), lambda i: (i, 0)),
        out_shape=jax.ShapeDtypeStruct((t, d), x.dtype),
        compiler_params=pltpu.CompilerParams(dimension_semantics=("parallel",)),
        name="rms_norm",
    )(x2, g.reshape(1, d).astype(F32))
    return out.reshape(shp)


def _mla_qkv_kernel(c_ref, qg_ref, kvg_ref, wq_ref, wkv_ref, cq_ref, sq_ref, ck_ref, sk_ref,
                    q_ref, k_ref, v_ref, *, n_heads, q_rank, kv_rank):
    c = c_ref[...]
    cqn = _rms(c[:, :q_rank], qg_ref[...]).astype(BF16)
    ckvn = _rms(c[:, q_rank:q_rank + kv_rank], kvg_ref[...]).astype(BF16)
    kr = c[:, q_rank + kv_rank:]
    k_hi = (kr * ck_ref[...] + pltpu.roll(kr, LANES // 2, axis=1) * sk_ref[...]).astype(BF16)
    tr = c.shape[0]
    lane = lax.broadcasted_iota(jnp.int32, (tr, LANES), 1)
    v_hi = jnp.where(lane == 0, 1.0, 0.0).astype(BF16)
    cq = cq_ref[...]
    sq = sq_ref[...]
    qw = 3 * LANES
    for h in range(n_heads):
        a = jnp.dot(cqn, wq_ref[:, h * qw:(h + 1) * qw], preferred_element_type=F32)
        q_ref[h, :, :LANES] = (a[:, :LANES] * ATTN_SCALE).astype(BF16)
        q_ref[h, :, LANES:] = (a[:, LANES:2 * LANES] * cq + a[:, 2 * LANES:] * sq).astype(BF16)
        kv = jnp.dot(ckvn, wkv_ref[:, h * HEAD_W:(h + 1) * HEAD_W], preferred_element_type=F32)
        k_ref[h, :, :LANES] = kv[:, :LANES].astype(BF16)
        k_ref[h, :, LANES:] = k_hi
        v_ref[h, :, :LANES] = kv[:, LANES:].astype(BF16)
        v_ref[h, :, LANES:] = v_hi


def _attn_kernel(q_ref, k_ref, v_ref, o_ref, *, tk, unroll):
    q = q_ref[...]
    tq = q.shape[0]
    nk = k_ref.shape[0] // tk

    def body(j, carry):
        m, acc = carry
        off = pl.multiple_of(j * tk, tk)
        s = lax.dot_general(q, k_ref[pl.ds(off, tk), :], (((1,), (1,)), ((), ())),
                            preferred_element_type=F32)
        m_new = jnp.maximum(m, jnp.max(s, axis=1, keepdims=True))
        p = jnp.exp((s - m_new).astype(BF16))
        acc = acc * jnp.exp(m - m_new) + jnp.dot(p, v_ref[pl.ds(off, tk), :],
                                                 preferred_element_type=F32)
        return m_new, acc

    m0 = jnp.full((tq, 1), -1e30, F32)
    acc0 = jnp.zeros((tq, HEAD_W), F32)
    _, acc = lax.fori_loop(0, nk, body, (m0, acc0), unroll=min(unroll, nk))
    o_ref[...] = (acc[:, :LANES] / acc[:, LANES:LANES + 1]).astype(o_ref.dtype)


def mla_attention(q, k, v, tq=512, tk=1024, unroll=16):
    b, nh, L, _ = q.shape
    tq = _row_tile(L, tq)
    tk = _row_tile(L, tk)
    kv_spec = pl.BlockSpec((None, None, L, HEAD_W), lambda bi, h, i: (bi, h, 0, 0))
    return pl.pallas_call(
        functools.partial(_attn_kernel, tk=tk, unroll=unroll),
        grid=(b, nh, L // tq),
        in_specs=[pl.BlockSpec((None, None, tq, HEAD_W), lambda bi, h, i: (bi, h, i, 0)),
                  kv_spec, kv_spec],
        out_specs=pl.BlockSpec((None, tq, V_DIM), lambda bi, h, i: (bi, i, h)),
        out_shape=jax.ShapeDtypeStruct((b, L, nh * V_DIM), BF16),
        compiler_params=pltpu.CompilerParams(
            dimension_semantics=("parallel", "parallel", "arbitrary"),
            vmem_limit_bytes=VMEM_LIMIT),
        name="mla_attention",
    )(q, k, v)


def _rot_cols(w):
    half = w.shape[-1] // 2
    return jnp.concatenate([-w[..., half:], w[..., :half]], axis=-1)


def mla_block(x, norm_g, w_in, q_norm, w_q_up, kv_norm, w_kv_up, w_out, tq=512, tk=1024):
    assert QK_NOPE == LANES and V_DIM == LANES and QK_ROPE == LANES // 2
    b, L, d = x.shape
    t = b * L
    nh = N_HEADS
    x2 = x.reshape(t, d)
    rk = Q_RANK + KV_RANK
    w_in_ext = jnp.concatenate([w_in, _rot_cols(w_in[:, rk:])], axis=1).astype(BF16)
    c = norm_matmul(x2, norm_g, w_in_ext)

    wq = w_q_up.reshape(Q_RANK, nh, QK_NOPE + QK_ROPE)
    rope_w = wq[..., QK_NOPE:]
    z = jnp.zeros_like(rope_w)
    wq3 = jnp.concatenate([wq[..., :QK_NOPE], rope_w, z, _rot_cols(rope_w), z], axis=-1)
    wq3 = wq3.reshape(Q_RANK, nh * 3 * LANES).astype(BF16)
    wkv = w_kv_up.astype(BF16)

    inv = 1.0 / (ROPE_THETA ** (jnp.arange(0, QK_ROPE, 2, dtype=F32) / QK_ROPE))
    ang = jnp.arange(L, dtype=F32)[:, None] * inv[None, :]
    ang = jnp.concatenate([ang, ang], axis=-1)
    zpad = jnp.zeros((L, LANES - QK_ROPE), F32)
    cos_k = jnp.concatenate([jnp.cos(ang), zpad], axis=1)
    sin_k = jnp.concatenate([jnp.sin(ang), zpad], axis=1)
    cos_q = cos_k * ATTN_SCALE
    sin_q = sin_k * ATTN_SCALE

    tr = _row_tile(L, 256)
    nl = L // tr
    cw = c.shape[1]
    tab = pl.BlockSpec((tr, LANES), lambda i: (i % nl, 0))
    slab = pl.BlockSpec((None, nh, tr, HEAD_W), lambda i: (i // nl, 0, i % nl, 0))
    slab_shape = jax.ShapeDtypeStruct((b, nh, L, HEAD_W), BF16)
    q, k, v = pl.pallas_call(
        functools.partial(_mla_qkv_kernel, n_heads=nh, q_rank=Q_RANK, kv_rank=KV_RANK),
        grid=(t // tr,),
        in_specs=[pl.BlockSpec((tr, cw), lambda i: (i, 0)),
                  pl.BlockSpec((1, Q_RANK), lambda i: (0, 0)),
                  pl.BlockSpec((1, KV_RANK), lambda i: (0, 0)),
                  pl.BlockSpec(wq3.shape, lambda i: (0, 0)),
                  pl.BlockSpec(wkv.shape, lambda i: (0, 0)),
                  tab, tab, tab, tab],
        out_specs=[slab, slab, slab],
        out_shape=[slab_shape, slab_shape, slab_shape],
        compiler_params=pltpu.CompilerParams(
            dimension_semantics=("parallel",), vmem_limit_bytes=VMEM_LIMIT),
        name="mla_qkv",
    )(c, q_norm.reshape(1, -1).astype(F32), kv_norm.reshape(1, -1).astype(F32), wq3, wkv,
      cos_q, sin_q, cos_k, sin_k)

    o = mla_attention(q, k, v, tq=tq, tk=tk)

    out = matmul_residual(o.reshape(t, nh * V_DIM), w_out.astype(BF16), x2)
    return out.reshape(b, L, d)


def _fft_split(L):
    n = 2 * L
    n2 = min(256, n // 16)
    return n // n2, n2


def _fft_tables(L):
    n1, n2 = _fft_split(L)
    n = n1 * n2
    n1h = n1 // 2
    k1n = n1 // 2 + 1
    k1p = min(n1, -(-k1n // 8) * 8)
    g = FFT_GROUP
    nb = n2 // g
    k1 = jnp.arange(k1p, dtype=jnp.int32)
    m1 = jnp.arange(n1h, dtype=jnp.int32)
    m2 = jnp.arange(n2, dtype=jnp.int32)
    live = (k1 < k1n).astype(F32)
    wgt = jnp.where((k1 == 0) | (k1 == n1 // 2), 1.0, 2.0) * live / n1
    th1 = ((k1[:, None] * m1[None, :]) % n1).astype(F32) * (2.0 * math.pi / n1)
    c1, s1 = jnp.cos(th1), jnp.sin(th1)
    eye = jnp.eye(g, dtype=F32)
    fa = jnp.concatenate([jnp.kron(c1 * live[:, None], eye), jnp.kron(-s1 * live[:, None], eye)],
                         axis=0).astype(BF16)
    gc = jnp.concatenate([jnp.kron((c1 * wgt[:, None]).T, eye), jnp.kron((-s1 * wgt[:, None]).T, eye)],
                         axis=1).astype(BF16)
    tht = ((k1[:, None] * m2[None, :]) % n).astype(F32) * (2.0 * math.pi / n)
    tw = jnp.stack([jnp.cos(tht), -jnp.sin(tht)], axis=0).reshape(2, k1p, nb, g)
    tw = tw.transpose(0, 2, 1, 3).reshape(2, nb, k1p * g, 1)
    idx2 = (m2[:, None] * m2[None, :]) % n2
    th2 = idx2.astype(F32) * (2.0 * math.pi / n2)
    c2, s2 = jnp.cos(th2), jnp.sin(th2)
    mf = jnp.concatenate([jnp.concatenate([c2, s2], 1), jnp.concatenate([-s2, c2], 1)], 0).astype(BF16)
    mi = (jnp.concatenate([jnp.concatenate([c2, -s2], 1), jnp.concatenate([s2, c2], 1)], 0) / n2).astype(BF16)
    return fa, gc, tw, mf, mi


def _fft_a_kernel(x_ref, fa_ref, tw_ref, y_ref):
    n1h, g, tc = x_ref.shape
    y = jnp.dot(fa_ref[...], x_ref[...].reshape(n1h * g, tc), preferred_element_type=F32)
    half = y.shape[0] // 2
    yr, yi = y[:half], y[half:]
    tr, ti = tw_ref[0], tw_ref[1]
    y_ref[0] = (yr * tr - yi * ti).reshape(y_ref.shape[1:]).astype(y_ref.dtype)
    y_ref[1] = (yr * ti + yi * tr).reshape(y_ref.shape[1:]).astype(y_ref.dtype)


def fft_stage_a(x, fa, tw, tc=512):
    s, L, d = x.shape
    rows, cols = fa.shape
    nb = tw.shape[1]
    g = FFT_GROUP
    n1h, k1p, n2 = cols // g, rows // (2 * g), nb * g
    tc = _row_tile(d, tc)
    return pl.pallas_call(
        _fft_a_kernel,
        grid=(s, nb, d // tc),
        in_specs=[pl.BlockSpec((None, n1h, g, tc), lambda b, j, c: (b, 0, j, c)),
                  pl.BlockSpec((rows, cols), lambda b, j, c: (0, 0)),
                  pl.BlockSpec((2, None, k1p * g, 1), lambda b, j, c: (0, j, 0, 0))],
        out_specs=pl.BlockSpec((None, 2, k1p, g, tc), lambda b, j, c: (b, 0, 0, j, c)),
        out_shape=jax.ShapeDtypeStruct((s, 2, k1p, n2, d), BF16),
        compiler_params=pltpu.CompilerParams(
            dimension_semantics=("parallel", "parallel", "parallel"), vmem_limit_bytes=VMEM_LIMIT),
        name="fft_stage_a",
    )(x.reshape(s, n1h, n2, d), fa, tw)


def _fft_b_kernel(y_ref, h_ref, mf_ref, mi_ref, u_ref):
    n2 = y_ref.shape[1]
    tc = y_ref.shape[2]
    z = jnp.dot(mf_ref[...], y_ref[...].reshape(2 * n2, tc), preferred_element_type=F32)
    zr, zi = z[:n2], z[n2:]
    hr, hi = h_ref[0].astype(F32), h_ref[1].astype(F32)
    w = jnp.concatenate([zr * hr - zi * hi, zr * hi + zi * hr], axis=0).astype(BF16)
    u = jnp.dot(mi_ref[...], w, preferred_element_type=F32)
    u_ref[0] = u[:n2].astype(u_ref.dtype)
    u_ref[1] = u[n2:].astype(u_ref.dtype)


def fft_stage_b(y, spec, order, mf, mi, tc=2048):
    s, _, k1p, n2, d = y.shape
    tc = _row_tile(d, tc)
    blk = pl.BlockSpec((None, 2, None, n2, tc), lambda k, j, b: (b, 0, k, 0, j))
    return pl.pallas_call(
        _fft_b_kernel,
        grid=(k1p, d // tc, s),
        in_specs=[blk,
                  pl.BlockSpec((None, 2, None, n2, tc), lambda k, j, b: (order, 0, k, 0, j)),
                  pl.BlockSpec(mf.shape, lambda k, j, b: (0, 0)),
                  pl.BlockSpec(mi.shape, lambda k, j, b: (0, 0))],
        out_specs=blk,
        out_shape=jax.ShapeDtypeStruct(y.shape, BF16),
        compiler_params=pltpu.CompilerParams(
            dimension_semantics=("parallel", "parallel", "arbitrary"), vmem_limit_bytes=VMEM_LIMIT),
        name="fft_stage_b",
    )(y, spec, mf, mi)


def _fft_c_kernel(u_ref, gc_ref, tw_ref, v_ref, g_ref, sk_ref, o_ref):
    _, k1p, g, tc = u_ref.shape
    rows = v_ref.shape[0] * g
    ur = u_ref[0].reshape(k1p * g, tc).astype(F32)
    ui = u_ref[1].reshape(k1p * g, tc).astype(F32)
    tr, ti = tw_ref[0], tw_ref[1]
    w = jnp.concatenate([ur * tr + ui * ti, ui * tr - ur * ti], axis=0).astype(BF16)
    y = jnp.dot(gc_ref[...], w, preferred_element_type=F32)
    v = v_ref[...].reshape(rows, tc).astype(F32)
    gate = g_ref[...].reshape(rows, tc).astype(F32)
    o_ref[...] = (gate * (y + v * sk_ref[...])).astype(o_ref.dtype).reshape(o_ref.shape)


def fft_stage_c(u, gc, tw, v, gate, skip, tc=512):
    s, _, k1p, n2, d = u.shape
    rows, cols = gc.shape
    g = FFT_GROUP
    nb = n2 // g
    n1h = rows // g
    L = n1h * n2
    tc = _row_tile(d, tc)
    row = pl.BlockSpec((None, n1h, g, tc), lambda b, j, c: (b, 0, j, c))
    out = pl.pallas_call(
        _fft_c_kernel,
        grid=(s, nb, d // tc),
        in_specs=[pl.BlockSpec((None, 2, k1p, g, tc), lambda b, j, c: (b, 0, 0, j, c)),
                  pl.BlockSpec((rows, cols), lambda b, j, c: (0, 0)),
                  pl.BlockSpec((2, None, k1p * g, 1), lambda b, j, c: (0, j, 0, 0)),
                  row, row,
                  pl.BlockSpec((1, tc), lambda b, j, c: (0, c))],
        out_specs=row,
        out_shape=jax.ShapeDtypeStruct((s, n1h, n2, d), BF16),
        compiler_params=pltpu.CompilerParams(
            dimension_semantics=("parallel", "parallel", "parallel"), vmem_limit_bytes=VMEM_LIMIT),
        name="fft_stage_c",
    )(u, gc, tw, v.reshape(s, n1h, n2, d), gate.reshape(s, n1h, n2, d), skip.reshape(1, d).astype(F32))
    return out.reshape(s, L, d)


def _filter_kernel(ft_ref, w1_ref, b1_ref, f1_ref, w2_ref, b2_ref, f2_ref, w3_ref, b3_ref, dl_ref,
                   taps_ref, sum_ref, *, seq_len):
    i = pl.program_id(1)
    tl = ft_ref.shape[0]
    a = jnp.sin(f1_ref[...] * (jnp.dot(ft_ref[...], w1_ref[...], preferred_element_type=F32) + b1_ref[...]))
    a = jnp.sin(f2_ref[...] * (jnp.dot(a, w2_ref[...], preferred_element_type=F32) + b2_ref[...]))
    f = jnp.dot(a, w3_ref[...], preferred_element_type=F32) + b3_ref[...]
    pos = (lax.broadcasted_iota(jnp.int32, (tl, 1), 0) + i * tl).astype(F32)
    t = pos * (1.0 / max(seq_len - 1, 1))
    f = f * jnp.exp(-t * dl_ref[...])

    @pl.when(i == 0)
    def _():
        sum_ref[...] = jnp.zeros_like(sum_ref)

    sum_ref[...] += jnp.sum(jnp.abs(f), axis=0, keepdims=True)
    is_bwd = pl.program_id(0) % 2 == 1
    f = jnp.where(jnp.logical_and(is_bwd, pos == 0.0), 0.0, f)
    taps_ref[...] = f.astype(taps_ref.dtype)


def hyena_filter_taps(L, d, f_w1, f_b1, f_fr1, f_w2, f_b2, f_fr2, f_w3, f_b3, tl=512):
    hp = LANES
    t = jnp.linspace(0.0, 1.0, L, dtype=F32)[:, None]
    w = (2.0 * math.pi / L) * jnp.arange(L, dtype=F32)[:, None]
    bands = jnp.linspace(1e-4, HY_BANDS - 1, HY_BANDS, dtype=F32)[None, :]
    feats = jnp.concatenate([t, jnp.cos(bands * w), -jnp.sin(bands * w),
                             jnp.zeros((L, hp - HY_EMB), F32)], axis=-1)

    def pad2(m, r, c):
        return jnp.pad(m.astype(F32), ((0, r - m.shape[0]), (0, c - m.shape[1])))

    w1 = pad2(f_w1, hp, hp)
    w2 = pad2(f_w2, hp, hp)
    w3 = pad2(f_w3, hp, f_w3.shape[1])
    b1, fr1 = pad2(f_b1[None], 1, hp), pad2(f_fr1[None], 1, hp)
    b2, fr2 = pad2(f_b2[None], 1, hp), pad2(f_fr2[None], 1, hp)
    max_decay = math.log(HY_TARGET) / HY_FAST_DECAY
    min_decay = math.log(HY_TARGET) / HY_SLOW_DECAY
    deltas = jnp.abs(jnp.linspace(min_decay, max_decay, d, dtype=F32))[None, :]
    ns = HY_ORDER * 2
    tl = _row_tile(L, tl)
    vec = lambda n: pl.BlockSpec((1, n), lambda s, i: (0, 0))
    mat = lambda r, c: pl.BlockSpec((r, c), lambda s, i: (0, 0))
    return pl.pallas_call(
        functools.partial(_filter_kernel, seq_len=L),
        grid=(ns, L // tl),
        in_specs=[pl.BlockSpec((tl, hp), lambda s, i: (i, 0)),
                  mat(hp, hp), vec(hp), vec(hp), mat(hp, hp), vec(hp), vec(hp),
                  pl.BlockSpec((hp, d), lambda s, i: (0, s)),
                  pl.BlockSpec((1, d), lambda s, i: (0, s)),
                  vec(d)],
        out_specs=[pl.BlockSpec((None, tl, d), lambda s, i: (s, i, 0)),
                   pl.BlockSpec((None, 1, d), lambda s, i: (s, 0, 0))],
        out_shape=[jax.ShapeDtypeStruct((ns, L, d), BF16), jax.ShapeDtypeStruct((ns, 1, d), F32)],
        compiler_params=pltpu.CompilerParams(dimension_semantics=("parallel", "arbitrary")),
        name="hyena_filter_taps",
    )(feats, w1, b1, fr1, w2, b2, fr2, w3, f_b3[None].astype(F32), deltas)


def _fft_bf_kernel(yf_ref, yb_ref, sf_ref, sb_ref, mf_ref, h_ref):
    n2 = yf_ref.shape[1]
    tc = yf_ref.shape[2]
    zf = jnp.dot(mf_ref[...], yf_ref[...].reshape(2 * n2, tc), preferred_element_type=F32)
    zb = jnp.dot(mf_ref[...], yb_ref[...].reshape(2 * n2, tc), preferred_element_type=F32)
    sf = 1.0 / (sf_ref[...] + 1e-6)
    sb = 1.0 / (sb_ref[...] + 1e-6)
    h_ref[0] = (zf[:n2] * sf + zb[:n2] * sb).astype(h_ref.dtype)
    h_ref[1] = (zf[n2:] * sf - zb[n2:] * sb).astype(h_ref.dtype)


def hyena_spectrum(L, d, fa, tw, mf, *filter_params, tc=1024):
    taps, sums = hyena_filter_taps(L, d, *filter_params)
    yv = fft_stage_a(taps, fa, tw)
    _, _, n1, n2, _ = yv.shape
    tc = _row_tile(d, tc)
    side = lambda sd: pl.BlockSpec((None, 2, None, n2, tc), lambda o, k, j: (2 * o + sd, 0, k, 0, j))
    norm = lambda sd: pl.BlockSpec((None, 1, tc), lambda o, k, j: (2 * o + sd, 0, j))
    return pl.pallas_call(
        _fft_bf_kernel,
        grid=(HY_ORDER, n1, d // tc),
        in_specs=[side(0), side(1), norm(0), norm(1), pl.BlockSpec(mf.shape, lambda o, k, j: (0, 0))],
        out_specs=pl.BlockSpec((None, 2, None, n2, tc), lambda o, k, j: (o, 0, k, 0, j)),
        out_shape=jax.ShapeDtypeStruct((HY_ORDER, 2, n1, n2, d), BF16),
        compiler_params=pltpu.CompilerParams(
            dimension_semantics=("parallel", "parallel", "parallel"), vmem_limit_bytes=VMEM_LIMIT),
        name="fft_stage_b_filter",
    )(yv, yv, sums, sums, mf)


def _shortconv_kernel(u_ref, up_ref, un_ref, w_ref, b_ref, v_ref, g1_ref, g2_ref):
    i = pl.program_id(1)
    tr = u_ref.shape[0]
    d = v_ref.shape[1]
    hr = up_ref.shape[0]
    first = i == 0
    last = i == pl.num_programs(1) - 1
    for j, o_ref in enumerate((v_ref, g1_ref, g2_ref)):
        cols = slice(j * d, (j + 1) * d)
        u = u_ref[:, cols].astype(F32)
        row = lax.broadcasted_iota(jnp.int32, u.shape, 0)
        prev_row = jnp.where(first, 0.0, up_ref[hr - 1:hr, cols].astype(F32))
        next_row = jnp.where(last, 0.0, un_ref[0:1, cols].astype(F32))
        um = jnp.where(row == 0, prev_row, pltpu.roll(u, 1, axis=0))
        up = jnp.where(row == tr - 1, next_row, pltpu.roll(u, tr - 1, axis=0))
        w = w_ref[:, cols]
        o_ref[...] = (um * w[0:1] + u * w[1:2] + up * w[2:3] + b_ref[:, cols]).astype(o_ref.dtype)


def hyena_shortconv(u, conv_w, conv_b, tr=256):
    b, L, d3 = u.shape
    d = d3 // 3
    tr = _row_tile(L, tr)
    hr = 16
    hb = tr // hr
    nhb = L // hr
    out = pl.BlockSpec((None, tr, d), lambda bi, i: (bi, i, 0))
    shape = jax.ShapeDtypeStruct((b, L, d), BF16)
    return pl.pallas_call(
        _shortconv_kernel,
        grid=(b, L // tr),
        in_specs=[pl.BlockSpec((None, tr, d3), lambda bi, i: (bi, i, 0)),
                  pl.BlockSpec((None, hr, d3), lambda bi, i: (bi, jnp.maximum(i * hb - 1, 0), 0)),
                  pl.BlockSpec((None, hr, d3), lambda bi, i: (bi, jnp.minimum((i + 1) * hb, nhb - 1), 0)),
                  pl.BlockSpec((3, d3), lambda bi, i: (0, 0)),
                  pl.BlockSpec((1, d3), lambda bi, i: (0, 0))],
        out_specs=[out, out, out],
        out_shape=[shape, shape, shape],
        compiler_params=pltpu.CompilerParams(dimension_semantics=("parallel", "parallel")),
        name="hyena_shortconv",
    )(u, u, u, conv_w.astype(F32), conv_b[None].astype(F32))


def hyena_block(x, norm_g, w_in, conv_w, conv_b, f_w1, f_b1, f_fr1, f_w2, f_b2, f_fr2, f_w3, f_b3,
                skip, w_out):
    b, L, d = x.shape
    t = b * L
    x2 = x.reshape(t, d)
    u = norm_matmul(x2, norm_g, w_in.astype(BF16), out_dtype=BF16).reshape(b, L, 3 * d)
    v, g1, g2 = hyena_shortconv(u, conv_w, conv_b)
    fa, gc, tw, mf, mi = _fft_tables(L)
    spec = hyena_spectrum(L, d, fa, tw, mf, f_w1, f_b1, f_fr1, f_w2, f_b2, f_fr2, f_w3, f_b3)
    z = fft_stage_c(fft_stage_b(fft_stage_a(v, fa, tw), spec, 0, mf, mi), gc, tw, v, g1, skip[0])
    y = fft_stage_c(fft_stage_b(fft_stage_a(z, fa, tw), spec, 1, mf, mi), gc, tw, z, g2, skip[1])
    out = matmul_residual(y.reshape(t, d), w_out.astype(BF16), x2)
    return out.reshape(b, L, d)


def _router_kernel(x_ref, g_ref, w_ref, lg_ref, hp_ref):
    h = _rms(x_ref[...].astype(F32), g_ref[...]).astype(BF16)
    lg_ref[...] = jnp.dot(h, w_ref[...], preferred_element_type=F32)
    bits = lax.bitcast_convert_type(h.astype(F32), jnp.uint32)
    half = bits.shape[1] // 2
    hp_ref[...] = bits[:, :half] | (bits[:, half:] >> 16)


def _unpack_rows(packed):
    hi = lax.bitcast_convert_type(packed & jnp.uint32(0xFFFF0000), F32)
    lo = lax.bitcast_convert_type(packed << 16, F32)
    return jnp.concatenate([hi, lo], axis=1).astype(BF16)


def _expert_kernel(be_ref, nu_ref, dst_ref, tok_ref, tok_next_ref, hp_hbm, wg_ref, wu_ref, wd_ref, rw_ref,
                   o_hbm, wg_s, wu_s, wd_s, x_s, y_s, gsem, sem, *, dump_row):
    i = pl.program_id(0)
    tm = rw_ref.shape[0]
    used = i < nu_ref[0]
    slot = i % 2

    def block_rows(s):
        return pltpu.make_async_copy(y_s.at[s], o_hbm.at[pl.ds(dump_row, tm), :], sem.at[s])

    def gathered_rows(s):
        return pltpu.make_async_copy(hp_hbm.at[pl.ds(0, tm), :], x_s.at[s], gsem.at[s])

    def start_gather(idx_ref, s):
        for r in range(tm):
            pltpu.make_async_copy(hp_hbm.at[pl.ds(idx_ref[0, r], 1), :],
                                  x_s.at[s, pl.ds(r, 1), :], gsem.at[s]).start()

    @pl.when(jnp.logical_and(i == 0, used))
    def _():
        start_gather(tok_ref, 0)

    @pl.when(i == nu_ref[0])
    def _():
        gathered_rows(slot).wait()

    @pl.when(i == 0)
    def _():
        y_s[1] = jnp.zeros(y_s.shape[1:], y_s.dtype)
        block_rows(1).start()
        block_rows(1).wait()

    @pl.when(jnp.logical_or(i == 0, be_ref[i] != be_ref[jnp.maximum(i - 1, 0)]))
    def _():
        wg_s[...] = wg_ref[...].astype(BF16)
        wu_s[...] = wu_ref[...].astype(BF16)
        wd_s[...] = wd_ref[...].astype(BF16)

    @pl.when(used)
    def _():
        start_gather(tok_next_ref, 1 - slot)
        gathered_rows(slot).wait()
        x = _unpack_rows(x_s[slot])
        g = jnp.dot(x, wg_s[...], preferred_element_type=F32)
        u = jnp.dot(x, wu_s[...], preferred_element_type=F32)
        a = (g * jax.nn.sigmoid(g) * u).astype(BF16)
        y = jnp.dot(a, wd_s[...], preferred_element_type=F32)
        y_s[slot] = y * rw_ref[...]

    @pl.when(jnp.logical_and(i >= 1, i - 1 < nu_ref[0]))
    def _():
        block_rows(1 - slot).wait()

    @pl.when(used)
    def _():
        for r in range(tm):
            pltpu.make_async_copy(y_s.at[slot, pl.ds(r, 1), :],
                                  o_hbm.at[pl.ds(dst_ref[0, r], 1), :], sem.at[slot]).start()

    @pl.when(jnp.logical_and(i == pl.num_programs(0) - 1, used))
    def _():
        block_rows(slot).wait()


def _moe_add_kernel(x_ref, a_ref, b_ref, o_ref):
    o_ref[...] = x_ref[...] + a_ref[...] + b_ref[...]


def moe_block(x, norm_g, w_rg, b_rg, w_re, b_re, w_gate, w_up, w_down, layer, tm=256):
    nb, L, d = x.shape
    T = nb * L
    ne = N_EXPERTS
    x2 = x.reshape(T, d)
    nr = N_GROUPS + ne
    w_r = jnp.concatenate([w_rg, w_re, jnp.zeros((d, LANES - nr), w_rg.dtype)], axis=1).astype(BF16)
    tr = _row_tile(T, 512)
    logits, hp = pl.pallas_call(
        _router_kernel,
        grid=(T // tr,),
        in_specs=[pl.BlockSpec((tr, d), lambda i: (i, 0)),
                  pl.BlockSpec((1, d), lambda i: (0, 0)),
                  pl.BlockSpec((d, LANES), lambda i: (0, 0))],
        out_specs=[pl.BlockSpec((tr, LANES), lambda i: (i, 0)),
                   pl.BlockSpec((tr, d // 2), lambda i: (i, 0))],
        out_shape=[jax.ShapeDtypeStruct((T, LANES), F32), jax.ShapeDtypeStruct((T, d // 2), jnp.uint32)],
        compiler_params=pltpu.CompilerParams(dimension_semantics=("parallel",)),
        name="moe_router",
    )(x2, norm_g.reshape(1, d).astype(F32), w_r)

    lg = logits[:, :N_GROUPS] + b_rg.astype(F32)
    pg = jax.nn.softmax(lg, axis=-1)
    gsel = jnp.argmax(lg, axis=-1).astype(jnp.int32)
    pg_sel = jnp.take_along_axis(pg, gsel[:, None], axis=1)
    le = (logits[:, N_GROUPS:nr] + b_re.astype(F32)).reshape(T, N_GROUPS, EXPERTS_PER_GROUP)
    le = jnp.take_along_axis(le, gsel[:, None, None], axis=1)[:, 0]
    pe = jax.nn.softmax(le, axis=-1)
    top_p, top_i = lax.top_k(pe, TOP_K)
    gate = pg_sel * top_p / jnp.sum(top_p, axis=-1, keepdims=True)
    eid = (gsel[:, None] * EXPERTS_PER_GROUP + top_i.astype(jnp.int32)).reshape(-1)
    gw = gate.reshape(-1)

    na = T * TOP_K
    order = jnp.argsort(eid).astype(jnp.int32)
    counts = jnp.bincount(eid, length=ne).astype(jnp.int32)
    start = jnp.cumsum(counts) - counts
    pcounts = (counts + tm - 1) // tm * tm
    pend = jnp.cumsum(pcounts)
    pstart = pend - pcounts
    n_rows = na + ne * tm
    n_blk = n_rows // tm
    blk_first = jnp.arange(n_blk, dtype=jnp.int32) * tm
    blk_e = jnp.minimum(jnp.sum((pend[None, :] <= blk_first[:, None]).astype(jnp.int32), axis=1), ne - 1)
    rep = lambda per_blk: jnp.repeat(per_blk, tm)
    rows = jnp.arange(n_rows, dtype=jnp.int32)
    off = rows - rep(pstart[blk_e])
    live = off < rep(counts[blk_e])
    row_a = order[jnp.clip(rep(start[blk_e]) + off, 0, na - 1)]
    row_tok = jnp.where(live, row_a // TOP_K, 0)
    row_w = jnp.where(live, gw[row_a], 0.0)
    row_dst = jnp.where(live, (row_a % TOP_K) * T + row_a // TOP_K, TOP_K * T + rows % tm)
    n_used = (pend[-1:] // tm).astype(jnp.int32)

    de = w_gate.shape[-1]
    idx_blk = lambda shift: pl.BlockSpec(
        (None, 1, tm), lambda i, be, nu: (jnp.minimum(i + shift, n_blk - 1), 0, 0), memory_space=pltpu.SMEM)
    w_in_blk = pl.BlockSpec((None, None, d, de), lambda i, be, nu: (layer, be[i], 0, 0))
    row_tok3 = row_tok.reshape(n_blk, 1, tm)
    y = pl.pallas_call(
        functools.partial(_expert_kernel, dump_row=TOP_K * T),
        grid_spec=pltpu.PrefetchScalarGridSpec(
            num_scalar_prefetch=2,
            grid=(n_blk,),
            in_specs=[idx_blk(0), idx_blk(0), idx_blk(1),
                      pl.BlockSpec(memory_space=pl.ANY),
                      w_in_blk, w_in_blk,
                      pl.BlockSpec((None, None, de, d), lambda i, be, nu: (layer, be[i], 0, 0)),
                      pl.BlockSpec((tm, 1), lambda i, be, nu: (i, 0))],
            out_specs=pl.BlockSpec(memory_space=pl.ANY),
            scratch_shapes=[pltpu.VMEM((d, de), BF16), pltpu.VMEM((d, de), BF16),
                            pltpu.VMEM((de, d), BF16), pltpu.VMEM((2, tm, d // 2), jnp.uint32),
                            pltpu.VMEM((2, tm, d), F32),
                            pltpu.SemaphoreType.DMA((2,)), pltpu.SemaphoreType.DMA((2,))]),
        out_shape=jax.ShapeDtypeStruct((TOP_K * T + tm, d), F32),
        compiler_params=pltpu.CompilerParams(
            dimension_semantics=("arbitrary",), vmem_limit_bytes=VMEM_LIMIT),
        name="moe_experts",
    )(blk_e, n_used, row_dst.reshape(n_blk, 1, tm), row_tok3, row_tok3, hp, w_gate, w_up, w_down,
      row_w[:, None])

    assert TOP_K == 2
    ta = _row_tile(T, 512)
    blk = lambda shift: pl.BlockSpec((ta, d), lambda i: (i + shift, 0))
    out = pl.pallas_call(
        _moe_add_kernel,
        grid=(T // ta,),
        in_specs=[blk(0), blk(0), blk(T // ta)],
        out_specs=blk(0),
        out_shape=jax.ShapeDtypeStruct((T, d), F32),
        compiler_params=pltpu.CompilerParams(dimension_semantics=("parallel",)),
        name="moe_combine",
    )(x2, y, y)
    return out.reshape(nb, L, d)


def encoder_trunk(x, norm_mix, norm_ffn, norm_final, mla_p, hy_p, moe_p):
    for i in range(DEPTH):
        j = i // 2
        if i % 2 == 0:
            x = mla_block(x, norm_mix[i], *[p[j] for p in mla_p])
        else:
            x = hyena_block(x, norm_mix[i], *[p[j] for p in hy_p])
        x = moe_block(x, norm_ffn[i], *[p[i] for p in moe_p[:4]], *moe_p[4:], layer=i)
    return rms_norm_pallas(x, norm_final)


def kernel(x_prompt, x_sample, norm_mix, norm_ffn, norm_final, mla_w_in, mla_q_norm, mla_w_q_up, mla_kv_norm, mla_w_kv_up, mla_w_out, hy_w_in, hy_conv_w, hy_conv_b, hy_f_w1, hy_f_b1, hy_f_freq1, hy_f_w2, hy_f_b2, hy_f_freq2, hy_f_w3, hy_f_b3, hy_skip, hy_w_out, moe_w_group, moe_b_group, moe_w_expert, moe_b_expert, moe_w_gate, moe_w_up, moe_w_down):
    mla_p = (mla_w_in, mla_q_norm, mla_w_q_up, mla_kv_norm, mla_w_kv_up, mla_w_out)
    hy_p = (hy_w_in, hy_conv_w, hy_conv_b, hy_f_w1, hy_f_b1, hy_f_freq1, hy_f_w2, hy_f_b2,
            hy_f_freq2, hy_f_w3, hy_f_b3, hy_skip, hy_w_out)
    moe_p = (moe_w_group, moe_b_group, moe_w_expert, moe_b_expert, moe_w_gate, moe_w_up, moe_w_down)
    y_prompt = encoder_trunk(x_prompt, norm_mix, norm_ffn, norm_final, mla_p, hy_p, moe_p)
    y_sample = encoder_trunk(x_sample, norm_mix, norm_ffn, norm_final, mla_p, hy_p, moe_p)
    return (y_prompt, y_sample)
```

```python
import functools
import math
import jax
import jax.numpy as jnp
from jax import lax
from jax.experimental import pallas as pl
from jax.experimental.pallas import tpu as pltpu

D_MODEL = 2048
DEPTH = 2
N_HEADS = 16
Q_RANK = 512
KV_RANK = 512
QK_NOPE = 128
QK_ROPE = 64
V_DIM = 128
ROPE_THETA = 10000.0
ATTN_SCALE = 1.0 / math.sqrt(QK_NOPE + QK_ROPE)
HY_ORDER = 2
HY_EMB = 33
HY_BANDS = (HY_EMB - 1) // 2
HY_HIDDEN = 64
HY_FAST_DECAY = 0.3
HY_SLOW_DECAY = 1.5
HY_TARGET = 1e-2
N_GROUPS = 8
EXPERTS_PER_GROUP = 8
N_EXPERTS = N_GROUPS * EXPERTS_PER_GROUP
TOP_K = 2
D_EXPERT = 512
ROW_BLOCK = 128
NORM_EPS = 1e-6

F32 = jnp.float32
BF16 = jnp.bfloat16
LANES = 128
HEAD_W = 2 * LANES
VMEM_LIMIT = 56 * 1024 * 1024
FFT_GROUP = 16


def _row_tile(n, want):
    t = min(n, want)
    while n % t:
        t //= 2
    return t


def _rms(x, g):
    return x * lax.rsqrt(jnp.mean(x * x, axis=-1, keepdims=True) + NORM_EPS) * g


def _norm_mm_kernel(x_ref, g_ref, w_ref, o_ref, h_scr):
    @pl.when(pl.program_id(1) == 0)
    def _():
        h_scr[...] = _rms(x_ref[...].astype(F32), g_ref[...]).astype(BF16)

    o_ref[...] = jnp.dot(h_scr[...], w_ref[...], preferred_element_type=F32).astype(o_ref.dtype)


def norm_matmul(x, g, w, out_dtype=F32, tm=512, tn=1024):
    m, k = x.shape
    n = w.shape[1]
    tm = _row_tile(m, tm)
    tn = n if n <= 2048 else _row_tile(n, tn)
    return pl.pallas_call(
        _norm_mm_kernel,
        grid=(m // tm, n // tn),
        in_specs=[pl.BlockSpec((tm, k), lambda i, j: (i, 0)),
                  pl.BlockSpec((1, k), lambda i, j: (0, 0)),
                  pl.BlockSpec((k, tn), lambda i, j: (0, j))],
        out_specs=pl.BlockSpec((tm, tn), lambda i, j: (i, j)),
        out_shape=jax.ShapeDtypeStruct((m, n), out_dtype),
        scratch_shapes=[pltpu.VMEM((tm, k), BF16)],
        compiler_params=pltpu.CompilerParams(
            dimension_semantics=("parallel", "arbitrary"), vmem_limit_bytes=VMEM_LIMIT),
        name="norm_matmul",
    )(x, g.reshape(1, k).astype(F32), w)


def _mm_res_kernel(a_ref, w_ref, r_ref, o_ref):
    acc = jnp.dot(a_ref[...].astype(BF16), w_ref[...], preferred_element_type=F32)
    o_ref[...] = (r_ref[...].astype(F32) + acc).astype(o_ref.dtype)


def matmul_residual(a, w, r, tm=512, tn=1024):
    m, k = a.shape
    n = w.shape[1]
    tm = _row_tile(m, tm)
    tn = _row_tile(n, tn)
    return pl.pallas_call(
        _mm_res_kernel,
        grid=(m // tm, n // tn),
        in_specs=[pl.BlockSpec((tm, k), lambda i, j: (i, 0)),
                  pl.BlockSpec((k, tn), lambda i, j: (0, j)),
                  pl.BlockSpec((tm, tn), lambda i, j: (i, j))],
        out_specs=pl.BlockSpec((tm, tn), lambda i, j: (i, j)),
        out_shape=jax.ShapeDtypeStruct((m, n), r.dtype),
        compiler_params=pltpu.CompilerParams(
            dimension_semantics=("parallel", "parallel"), vmem_limit_bytes=VMEM_LIMIT),
        name="matmul_residual",
    )(a, w, r)


def _rms_kernel(x_ref, g_ref, o_ref):
    o_ref[...] = _rms(x_ref[...].astype(F32), g_ref[...]).astype(o_ref.dtype)


def rms_norm_pallas(x, g, rows=512):
    shp = x.shape
    d = shp[-1]
    x2 = x.reshape(-1, d)
    t = x2.shape[0]
    rows = _row_tile(t, rows)
    out = pl.pallas_call(
        _rms_kernel,
        grid=(t // rows,),
        in_specs=[pl.BlockSpec((rows, d), lambda i: (i, 0)),
                  pl.BlockSpec((1, d), lambda i: (0, 0))],
        out_specs=pl.BlockSpec((rows, d), lambda i: (i, 0)),
        out_shape=jax.ShapeDtypeStruct((t, d), x.dtype),
        compiler_params=pltpu.CompilerParams(dimension_semantics=("parallel",)),
        name="rms_norm",
    )(x2, g.reshape(1, d).astype(F32))
    return out.reshape(shp)


def _mla_qkv_kernel(c_ref, qg_ref, kvg_ref, wq_ref, wkv_ref, cq_ref, sq_ref, ck_ref, sk_ref,
                    q_ref, k_ref, v_ref, *, n_heads, q_rank, kv_rank):
    c = c_ref[...]
    cqn = _rms(c[:, :q_rank], qg_ref[...]).astype(BF16)
    ckvn = _rms(c[:, q_rank:q_rank + kv_rank], kvg_ref[...]).astype(BF16)
    kr = c[:, q_rank + kv_rank:]
    k_hi = (kr * ck_ref[...] + pltpu.roll(kr, LANES // 2, axis=1) * sk_ref[...]).astype(BF16)
    tr = c.shape[0]
    lane = lax.broadcasted_iota(jnp.int32, (tr, LANES), 1)
    v_hi = jnp.where(lane == 0, 1.0, 0.0).astype(BF16)
    cq = cq_ref[...]
    sq = sq_ref[...]
    qw = 3 * LANES
    for h in range(n_heads):
        a = jnp.dot(cqn, wq_ref[:, h * qw:(h + 1) * qw], preferred_element_type=F32)
        q_ref[h, :, :LANES] = (a[:, :LANES] * ATTN_SCALE).astype(BF16)
        q_ref[h, :, LANES:] = (a[:, LANES:2 * LANES] * cq + a[:, 2 * LANES:] * sq).astype(BF16)
        kv = jnp.dot(ckvn, wkv_ref[:, h * HEAD_W:(h + 1) * HEAD_W], preferred_element_type=F32)
        k_ref[h, :, :LANES] = kv[:, :LANES].astype(BF16)
        k_ref[h, :, LANES:] = k_hi
        v_ref[h, :, :LANES] = kv[:, LANES:].astype(BF16)
        v_ref[h, :, LANES:] = v_hi


def _attn_kernel(q_ref, k_ref, v_ref, o_ref, *, tk, unroll):
    q = q_ref[...]
    tq = q.shape[0]
    nk = k_ref.shape[0] // tk

    def body(j, carry):
        m, acc = carry
        off = pl.multiple_of(j * tk, tk)
        s = lax.dot_general(q, k_ref[pl.ds(off, tk), :], (((1,), (1,)), ((), ())),
                            preferred_element_type=F32)
        m_new = jnp.maximum(m, jnp.max(s, axis=1, keepdims=True))
        p = jnp.exp((s - m_new).astype(BF16))
        acc = acc * jnp.exp(m - m_new) + jnp.dot(p, v_ref[pl.ds(off, tk), :],
                                                 preferred_element_type=F32)
        return m_new, acc

    m0 = jnp.full((tq, 1), -1e30, F32)
    acc0 = jnp.zeros((tq, HEAD_W), F32)
    _, acc = lax.fori_loop(0, nk, body, (m0, acc0), unroll=min(unroll, nk))
    o_ref[...] = (acc[:, :LANES] / acc[:, LANES:LANES + 1]).astype(o_ref.dtype)


def mla_attention(q, k, v, tq=512, tk=1024, unroll=16):
    b, nh, L, _ = q.shape
    tq = _row_tile(L, tq)
    tk = _row_tile(L, tk)
    kv_spec = pl.BlockSpec((None, None, L, HEAD_W), lambda bi, h, i: (bi, h, 0, 0))
    return pl.pallas_call(
        functools.partial(_attn_kernel, tk=tk, unroll=unroll),
        grid=(b, nh, L // tq),
        in_specs=[pl.BlockSpec((None, None, tq, HEAD_W), lambda bi, h, i: (bi, h, i, 0)),
                  kv_spec, kv_spec],
        out_specs=pl.BlockSpec((None, tq, V_DIM), lambda bi, h, i: (bi, i, h)),
        out_shape=jax.ShapeDtypeStruct((b, L, nh * V_DIM), BF16),
        compiler_params=pltpu.CompilerParams(
            dimension_semantics=("parallel", "parallel", "arbitrary"),
            vmem_limit_bytes=VMEM_LIMIT),
        name="mla_attention",
    )(q, k, v)


def _rot_cols(w):
    half = w.shape[-1] // 2
    return jnp.concatenate([-w[..., half:], w[..., :half]], axis=-1)


def mla_block(x, norm_g, w_in, q_norm, w_q_up, kv_norm, w_kv_up, w_out, tq=512, tk=1024):
    assert QK_NOPE == LANES and V_DIM == LANES and QK_ROPE == LANES // 2
    b, L, d = x.shape
    t = b * L
    nh = N_HEADS
    x2 = x.reshape(t, d)
    rk = Q_RANK + KV_RANK
    w_in_ext = jnp.concatenate([w_in, _rot_cols(w_in[:, rk:])], axis=1).astype(BF16)
    c = norm_matmul(x2, norm_g, w_in_ext)

    wq = w_q_up.reshape(Q_RANK, nh, QK_NOPE + QK_ROPE)
    rope_w = wq[..., QK_NOPE:]
    z = jnp.zeros_like(rope_w)
    wq3 = jnp.concatenate([wq[..., :QK_NOPE], rope_w, z, _rot_cols(rope_w), z], axis=-1)
    wq3 = wq3.reshape(Q_RANK, nh * 3 * LANES).astype(BF16)
    wkv = w_kv_up.astype(BF16)

    inv = 1.0 / (ROPE_THETA ** (jnp.arange(0, QK_ROPE, 2, dtype=F32) / QK_ROPE))
    ang = jnp.arange(L, dtype=F32)[:, None] * inv[None, :]
    ang = jnp.concatenate([ang, ang], axis=-1)
    zpad = jnp.zeros((L, LANES - QK_ROPE), F32)
    cos_k = jnp.concatenate([jnp.cos(ang), zpad], axis=1)
    sin_k = jnp.concatenate([jnp.sin(ang), zpad], axis=1)
    cos_q = cos_k * ATTN_SCALE
    sin_q = sin_k * ATTN_SCALE

    tr = _row_tile(L, 256)
    nl = L // tr
    cw = c.shape[1]
    tab = pl.BlockSpec((tr, LANES), lambda i: (i % nl, 0))
    slab = pl.BlockSpec((None, nh, tr, HEAD_W), lambda i: (i // nl, 0, i % nl, 0))
    slab_shape = jax.ShapeDtypeStruct((b, nh, L, HEAD_W), BF16)
    q, k, v = pl.pallas_call(
        functools.partial(_mla_qkv_kernel, n_heads=nh, q_rank=Q_RANK, kv_rank=KV_RANK),
        grid=(t // tr,),
        in_specs=[pl.BlockSpec((tr, cw), lambda i: (i, 0)),
                  pl.BlockSpec((1, Q_RANK), lambda i: (0, 0)),
                  pl.BlockSpec((1, KV_RANK), lambda i: (0, 0)),
                  pl.BlockSpec(wq3.shape, lambda i: (0, 0)),
                  pl.BlockSpec(wkv.shape, lambda i: (0, 0)),
                  tab, tab, tab, tab],
        out_specs=[slab, slab, slab],
        out_shape=[slab_shape, slab_shape, slab_shape],
        compiler_params=pltpu.CompilerParams(
            dimension_semantics=("parallel",), vmem_limit_bytes=VMEM_LIMIT),
        name="mla_qkv",
    )(c, q_norm.reshape(1, -1).astype(F32), kv_norm.reshape(1, -1).astype(F32), wq3, wkv,
      cos_q, sin_q, cos_k, sin_k)

    o = mla_attention(q, k, v, tq=tq, tk=tk)

    out = matmul_residual(o.reshape(t, nh * V_DIM), w_out.astype(BF16), x2)
    return out.reshape(b, L, d)


def _fft_split(L):
    n = 2 * L
    n2 = min(256, n // 16)
    return n // n2, n2


def _fft_tables(L):
    n1, n2 = _fft_split(L)
    n = n1 * n2
    n1h = n1 // 2
    k1n = n1 // 2 + 1
    k1p = min(n1, -(-k1n // 8) * 8)
    g = FFT_GROUP
    nb = n2 // g
    k1 = jnp.arange(k1p, dtype=jnp.int32)
    m1 = jnp.arange(n1h, dtype=jnp.int32)
    m2 = jnp.arange(n2, dtype=jnp.int32)
    live = (k1 < k1n).astype(F32)
    wgt = jnp.where((k1 == 0) | (k1 == n1 // 2), 1.0, 2.0) * live / n1
    th1 = ((k1[:, None] * m1[None, :]) % n1).astype(F32) * (2.0 * math.pi / n1)
    c1, s1 = jnp.cos(th1), jnp.sin(th1)
    eye = jnp.eye(g, dtype=F32)
    fa = jnp.concatenate([jnp.kron(c1 * live[:, None], eye), jnp.kron(-s1 * live[:, None], eye)],
                         axis=0).astype(BF16)
    gc = jnp.concatenate([jnp.kron((c1 * wgt[:, None]).T, eye), jnp.kron((-s1 * wgt[:, None]).T, eye)],
                         axis=1).astype(BF16)
    tht = ((k1[:, None] * m2[None, :]) % n).astype(F32) * (2.0 * math.pi / n)
    tw = jnp.stack([jnp.cos(tht), -jnp.sin(tht)], axis=0).reshape(2, k1p, nb, g)
    tw = tw.transpose(0, 2, 1, 3).reshape(2, nb, k1p * g, 1)
    idx2 = (m2[:, None] * m2[None, :]) % n2
    th2 = idx2.astype(F32) * (2.0 * math.pi / n2)
    c2, s2 = jnp.cos(th2), jnp.sin(th2)
    mf = jnp.concatenate([jnp.concatenate([c2, s2], 1), jnp.concatenate([-s2, c2], 1)], 0).astype(BF16)
    mi = (jnp.concatenate([jnp.concatenate([c2, -s2], 1), jnp.concatenate([s2, c2], 1)], 0) / n2).astype(BF16)
    return fa, gc, tw, mf, mi


def _fft_a_kernel(x_ref, fa_ref, tw_ref, y_ref):
    n1h, g, tc = x_ref.shape
    y = jnp.dot(fa_ref[...], x_ref[...].reshape(n1h * g, tc), preferred_element_type=F32)
    half = y.shape[0] // 2
    yr, yi = y[:half], y[half:]
    tr, ti = tw_ref[0], tw_ref[1]
    y_ref[0] = (yr * tr - yi * ti).reshape(y_ref.shape[1:]).astype(y_ref.dtype)
    y_ref[1] = (yr * ti + yi * tr).reshape(y_ref.shape[1:]).astype(y_ref.dtype)


def fft_stage_a(x, fa, tw, tc=512):
    s, L, d = x.shape
    rows, cols = fa.shape
    nb = tw.shape[1]
    g = FFT_GROUP
    n1h, k1p, n2 = cols // g, rows // (2 * g), nb * g
    tc = _row_tile(d, tc)
    return pl.pallas_call(
        _fft_a_kernel,
        grid=(s, nb, d // tc),
        in_specs=[pl.BlockSpec((None, n1h, g, tc), lambda b, j, c: (b, 0, j, c)),
                  pl.BlockSpec((rows, cols), lambda b, j, c: (0, 0)),
                  pl.BlockSpec((2, None, k1p * g, 1), lambda b, j, c: (0, j, 0, 0))],
        out_specs=pl.BlockSpec((None, 2, k1p, g, tc), lambda b, j, c: (b, 0, 0, j, c)),
        out_shape=jax.ShapeDtypeStruct((s, 2, k1p, n2, d), BF16),
        compiler_params=pltpu.CompilerParams(
            dimension_semantics=("parallel", "parallel", "parallel"), vmem_limit_bytes=VMEM_LIMIT),
        name="fft_stage_a",
    )(x.reshape(s, n1h, n2, d), fa, tw)


def _fft_b_kernel(y_ref, h_ref, mf_ref, mi_ref, u_ref):
    n2 = y_ref.shape[1]
    tc = y_ref.shape[2]
    z = jnp.dot(mf_ref[...], y_ref[...].reshape(2 * n2, tc), preferred_element_type=F32)
    zr, zi = z[:n2], z[n2:]
    hr, hi = h_ref[0].astype(F32), h_ref[1].astype(F32)
    w = jnp.concatenate([zr * hr - zi * hi, zr * hi + zi * hr], axis=0).astype(BF16)
    u = jnp.dot(mi_ref[...], w, preferred_element_type=F32)
    u_ref[0] = u[:n2].astype(u_ref.dtype)
    u_ref[1] = u[n2:].astype(u_ref.dtype)


def fft_stage_b(y, spec, order, mf, mi, tc=2048):
    s, _, k1p, n2, d = y.shape
    tc = _row_tile(d, tc)
    blk = pl.BlockSpec((None, 2, None, n2, tc), lambda k, j, b: (b, 0, k, 0, j))
    return pl.pallas_call(
        _fft_b_kernel,
        grid=(k1p, d // tc, s),
        in_specs=[blk,
                  pl.BlockSpec((None, 2, None, n2, tc), lambda k, j, b: (order, 0, k, 0, j)),
                  pl.BlockSpec(mf.shape, lambda k, j, b: (0, 0)),
                  pl.BlockSpec(mi.shape, lambda k, j, b: (0, 0))],
        out_specs=blk,
        out_shape=jax.ShapeDtypeStruct(y.shape, BF16),
        compiler_params=pltpu.CompilerParams(
            dimension_semantics=("parallel", "parallel", "arbitrary"), vmem_limit_bytes=VMEM_LIMIT),
        name="fft_stage_b",
    )(y, spec, mf, mi)


def _fft_c_kernel(u_ref, gc_ref, tw_ref, v_ref, g_ref, sk_ref, o_ref):
    _, k1p, g, tc = u_ref.shape
    rows = v_ref.shape[0] * g
    ur = u_ref[0].reshape(k1p * g, tc).astype(F32)
    ui = u_ref[1].reshape(k1p * g, tc).astype(F32)
    tr, ti = tw_ref[0], tw_ref[1]
    w = jnp.concatenate([ur * tr + ui * ti, ui * tr - ur * ti], axis=0).astype(BF16)
    y = jnp.dot(gc_ref[...], w, preferred_element_type=F32)
    v = v_ref[...].reshape(rows, tc).astype(F32)
    gate = g_ref[...].reshape(rows, tc).astype(F32)
    o_ref[...] = (gate * (y + v * sk_ref[...])).astype(o_ref.dtype).reshape(o_ref.shape)


def fft_stage_c(u, gc, tw, v, gate, skip, tc=512):
    s, _, k1p, n2, d = u.shape
    rows, cols = gc.shape
    g = FFT_GROUP
    nb = n2 // g
    n1h = rows // g
    L = n1h * n2
    tc = _row_tile(d, tc)
    row = pl.BlockSpec((None, n1h, g, tc), lambda b, j, c: (b, 0, j, c))
    out = pl.pallas_call(
        _fft_c_kernel,
        grid=(s, nb, d // tc),
        in_specs=[pl.BlockSpec((None, 2, k1p, g, tc), lambda b, j, c: (b, 0, 0, j, c)),
                  pl.BlockSpec((rows, cols), lambda b, j, c: (0, 0)),
                  pl.BlockSpec((2, None, k1p * g, 1), lambda b, j, c: (0, j, 0, 0)),
                  row, row,
                  pl.BlockSpec((1, tc), lambda b, j, c: (0, c))],
        out_specs=row,
        out_shape=jax.ShapeDtypeStruct((s, n1h, n2, d), BF16),
        compiler_params=pltpu.CompilerParams(
            dimension_semantics=("parallel", "parallel", "parallel"), vmem_limit_bytes=VMEM_LIMIT),
        name="fft_stage_c",
    )(u, gc, tw, v.reshape(s, n1h, n2, d), gate.reshape(s, n1h, n2, d), skip.reshape(1, d).astype(F32))
    return out.reshape(s, L, d)


def _filter_kernel(ft_ref, w1_ref, b1_ref, f1_ref, w2_ref, b2_ref, f2_ref, w3_ref, b3_ref, dl_ref,
                   taps_ref, sum_ref, *, seq_len):
    i = pl.program_id(1)
    tl = ft_ref.shape[0]
    a = jnp.sin(f1_ref[...] * (jnp.dot(ft_ref[...], w1_ref[...], preferred_element_type=F32) + b1_ref[...]))
    a = jnp.sin(f2_ref[...] * (jnp.dot(a, w2_ref[...], preferred_element_type=F32) + b2_ref[...]))
    f = jnp.dot(a, w3_ref[...], preferred_element_type=F32) + b3_ref[...]
    pos = (lax.broadcasted_iota(jnp.int32, (tl, 1), 0) + i * tl).astype(F32)
    t = pos * (1.0 / max(seq_len - 1, 1))
    f = f * jnp.exp(-t * dl_ref[...])

    @pl.when(i == 0)
    def _():
        sum_ref[...] = jnp.zeros_like(sum_ref)

    sum_ref[...] += jnp.sum(jnp.abs(f), axis=0, keepdims=True)
    is_bwd = pl.program_id(0) % 2 == 1
    f = jnp.where(jnp.logical_and(is_bwd, pos == 0.0), 0.0, f)
    taps_ref[...] = f.astype(taps_ref.dtype)


def hyena_filter_taps(L, d, f_w1, f_b1, f_fr1, f_w2, f_b2, f_fr2, f_w3, f_b3, tl=512):
    hp = LANES
    t = jnp.linspace(0.0, 1.0, L, dtype=F32)[:, None]
    w = (2.0 * math.pi / L) * jnp.arange(L, dtype=F32)[:, None]
    bands = jnp.linspace(1e-4, HY_BANDS - 1, HY_BANDS, dtype=F32)[None, :]
    feats = jnp.concatenate([t, jnp.cos(bands * w), -jnp.sin(bands * w),
                             jnp.zeros((L, hp - HY_EMB), F32)], axis=-1)

    def pad2(m, r, c):
        return jnp.pad(m.astype(F32), ((0, r - m.shape[0]), (0, c - m.shape[1])))

    w1 = pad2(f_w1, hp, hp)
    w2 = pad2(f_w2, hp, hp)
    w3 = pad2(f_w3, hp, f_w3.shape[1])
    b1, fr1 = pad2(f_b1[None], 1, hp), pad2(f_fr1[None], 1, hp)
    b2, fr2 = pad2(f_b2[None], 1, hp), pad2(f_fr2[None], 1, hp)
    max_decay = math.log(HY_TARGET) / HY_FAST_DECAY
    min_decay = math.log(HY_TARGET) / HY_SLOW_DECAY
    deltas = jnp.abs(jnp.linspace(min_decay, max_decay, d, dtype=F32))[None, :]
    ns = HY_ORDER * 2
    tl = _row_tile(L, tl)
    vec = lambda n: pl.BlockSpec((1, n), lambda s, i: (0, 0))
    mat = lambda r, c: pl.BlockSpec((r, c), lambda s, i: (0, 0))
    return pl.pallas_call(
        functools.partial(_filter_kernel, seq_len=L),
        grid=(ns, L // tl),
        in_specs=[pl.BlockSpec((tl, hp), lambda s, i: (i, 0)),
                  mat(hp, hp), vec(hp), vec(hp), mat(hp, hp), vec(hp), vec(hp),
                  pl.BlockSpec((hp, d), lambda s, i: (0, s)),
                  pl.BlockSpec((1, d), lambda s, i: (0, s)),
                  vec(d)],
        out_specs=[pl.BlockSpec((None, tl, d), lambda s, i: (s, i, 0)),
                   pl.BlockSpec((None, 1, d), lambda s, i: (s, 0, 0))],
        out_shape=[jax.ShapeDtypeStruct((ns, L, d), BF16), jax.ShapeDtypeStruct((ns, 1, d), F32)],
        compiler_params=pltpu.CompilerParams(dimension_semantics=("parallel", "arbitrary")),
        name="hyena_filter_taps",
    )(feats, w1, b1, fr1, w2, b2, fr2, w3, f_b3[None].astype(F32), deltas)


def _fft_bf_kernel(yf_ref, yb_ref, sf_ref, sb_ref, mf_ref, h_ref):
    n2 = yf_ref.shape[1]
    tc = yf_ref.shape[2]
    zf = jnp.dot(mf_ref[...], yf_ref[...].reshape(2 * n2, tc), preferred_element_type=F32)
    zb = jnp.dot(mf_ref[...], yb_ref[...].reshape(2 * n2, tc), preferred_element_type=F32)
    sf = 1.0 / (sf_ref[...] + 1e-6)
    sb = 1.0 / (sb_ref[...] + 1e-6)
    h_ref[0] = (zf[:n2] * sf + zb[:n2] * sb).astype(h_ref.dtype)
    h_ref[1] = (zf[n2:] * sf - zb[n2:] * sb).astype(h_ref.dtype)


def hyena_spectrum(L, d, fa, tw, mf, *filter_params, tc=1024):
    taps, sums = hyena_filter_taps(L, d, *filter_params)
    yv = fft_stage_a(taps, fa, tw)
    _, _, n1, n2, _ = yv.shape
    tc = _row_tile(d, tc)
    side = lambda sd: pl.BlockSpec((None, 2, None, n2, tc), lambda o, k, j: (2 * o + sd, 0, k, 0, j))
    norm = lambda sd: pl.BlockSpec((None, 1, tc), lambda o, k, j: (2 * o + sd, 0, j))
    return pl.pallas_call(
        _fft_bf_kernel,
        grid=(HY_ORDER, n1, d // tc),
        in_specs=[side(0), side(1), norm(0), norm(1), pl.BlockSpec(mf.shape, lambda o, k, j: (0, 0))],
        out_specs=pl.BlockSpec((None, 2, None, n2, tc), lambda o, k, j: (o, 0, k, 0, j)),
        out_shape=jax.ShapeDtypeStruct((HY_ORDER, 2, n1, n2, d), BF16),
        compiler_params=pltpu.CompilerParams(
            dimension_semantics=("parallel", "parallel", "parallel"), vmem_limit_bytes=VMEM_LIMIT),
        name="fft_stage_b_filter",
    )(yv, yv, sums, sums, mf)


def _shortconv_kernel(u_ref, up_ref, un_ref, w_ref, b_ref, v_ref, g1_ref, g2_ref):
    i = pl.program_id(1)
    tr = u_ref.shape[0]
    d = v_ref.shape[1]
    hr = up_ref.shape[0]
    first = i == 0
    last = i == pl.num_programs(1) - 1
    for j, o_ref in enumerate((v_ref, g1_ref, g2_ref)):
        cols = slice(j * d, (j + 1) * d)
        u = u_ref[:, cols].astype(F32)
        row = lax.broadcasted_iota(jnp.int32, u.shape, 0)
        prev_row = jnp.where(first, 0.0, up_ref[hr - 1:hr, cols].astype(F32))
        next_row = jnp.where(last, 0.0, un_ref[0:1, cols].astype(F32))
        um = jnp.where(row == 0, prev_row, pltpu.roll(u, 1, axis=0))
        up = jnp.where(row == tr - 1, next_row, pltpu.roll(u, tr - 1, axis=0))
        w = w_ref[:, cols]
        o_ref[...] = (um * w[0:1] + u * w[1:2] + up * w[2:3] + b_ref[:, cols]).astype(o_ref.dtype)


def hyena_shortconv(u, conv_w, conv_b, tr=256):
    b, L, d3 = u.shape
    d = d3 // 3
    tr = _row_tile(L, tr)
    hr = 16
    hb = tr // hr
    nhb = L // hr
    out = pl.BlockSpec((None, tr, d), lambda bi, i: (bi, i, 0))
    shape = jax.ShapeDtypeStruct((b, L, d), BF16)
    return pl.pallas_call(
        _shortconv_kernel,
        grid=(b, L // tr),
        in_specs=[pl.BlockSpec((None, tr, d3), lambda bi, i: (bi, i, 0)),
                  pl.BlockSpec((None, hr, d3), lambda bi, i: (bi, jnp.maximum(i * hb - 1, 0), 0)),
                  pl.BlockSpec((None, hr, d3), lambda bi, i: (bi, jnp.minimum((i + 1) * hb, nhb - 1), 0)),
                  pl.BlockSpec((3, d3), lambda bi, i: (0, 0)),
                  pl.BlockSpec((1, d3), lambda bi, i: (0, 0))],
        out_specs=[out, out, out],
        out_shape=[shape, shape, shape],
        compiler_params=pltpu.CompilerParams(dimension_semantics=("parallel", "parallel")),
        name="hyena_shortconv",
    )(u, u, u, conv_w.astype(F32), conv_b[None].astype(F32))


def hyena_block(x, norm_g, w_in, conv_w, conv_b, f_w1, f_b1, f_fr1, f_w2, f_b2, f_fr2, f_w3, f_b3,
                skip, w_out):
    b, L, d = x.shape
    t = b * L
    x2 = x.reshape(t, d)
    u = norm_matmul(x2, norm_g, w_in.astype(BF16), out_dtype=BF16).reshape(b, L, 3 * d)
    v, g1, g2 = hyena_shortconv(u, conv_w, conv_b)
    fa, gc, tw, mf, mi = _fft_tables(L)
    spec = hyena_spectrum(L, d, fa, tw, mf, f_w1, f_b1, f_fr1, f_w2, f_b2, f_fr2, f_w3, f_b3)
    z = fft_stage_c(fft_stage_b(fft_stage_a(v, fa, tw), spec, 0, mf, mi), gc, tw, v, g1, skip[0])
    y = fft_stage_c(fft_stage_b(fft_stage_a(z, fa, tw), spec, 1, mf, mi), gc, tw, z, g2, skip[1])
    out = matmul_residual(y.reshape(t, d), w_out.astype(BF16), x2)
    return out.reshape(b, L, d)


def _router_kernel(x_ref, g_ref, w_ref, lg_ref, hp_ref):
    h = _rms(x_ref[...].astype(F32), g_ref[...]).astype(BF16)
    lg_ref[...] = jnp.dot(h, w_ref[...], preferred_element_type=F32)
    bits = lax.bitcast_convert_type(h.astype(F32), jnp.uint32)
    half = bits.shape[1] // 2
    hp_ref[...] = bits[:, :half] | (bits[:, half:] >> 16)


def _unpack_rows(packed):
    hi = lax.bitcast_convert_type(packed & jnp.uint32(0xFFFF0000), F32)
    lo = lax.bitcast_convert_type(packed << 16, F32)
    return jnp.concatenate([hi, lo], axis=1).astype(BF16)


def _expert_kernel(be_ref, nu_ref, dst_ref, tok_ref, tok_next_ref, hp_hbm, wg_ref, wu_ref, wd_ref, rw_ref,
                   o_hbm, wg_s, wu_s, wd_s, x_s, y_s, gsem, sem, *, dump_row):
    i = pl.program_id(0)
    tm = rw_ref.shape[0]
    used = i < nu_ref[0]
    slot = i % 2

    def block_rows(s):
        return pltpu.make_async_copy(y_s.at[s], o_hbm.at[pl.ds(dump_row, tm), :], sem.at[s])

    def gathered_rows(s):
        return pltpu.make_async_copy(hp_hbm.at[pl.ds(0, tm), :], x_s.at[s], gsem.at[s])

    def start_gather(idx_ref, s):
        for r in range(tm):
            pltpu.make_async_copy(hp_hbm.at[pl.ds(idx_ref[0, r], 1), :],
                                  x_s.at[s, pl.ds(r, 1), :], gsem.at[s]).start()

    @pl.when(jnp.logical_and(i == 0, used))
    def _():
        start_gather(tok_ref, 0)

    @pl.when(i == nu_ref[0])
    def _():
        gathered_rows(slot).wait()

    @pl.when(i == 0)
    def _():
        y_s[1] = jnp.zeros(y_s.shape[1:], y_s.dtype)
        block_rows(1).start()
        block_rows(1).wait()

    @pl.when(jnp.logical_or(i == 0, be_ref[i] != be_ref[jnp.maximum(i - 1, 0)]))
    def _():
        wg_s[...] = wg_ref[...].astype(BF16)
        wu_s[...] = wu_ref[...].astype(BF16)
        wd_s[...] = wd_ref[...].astype(BF16)

    @pl.when(used)
    def _():
        start_gather(tok_next_ref, 1 - slot)
        gathered_rows(slot).wait()
        x = _unpack_rows(x_s[slot])
        g = jnp.dot(x, wg_s[...], preferred_element_type=F32)
        u = jnp.dot(x, wu_s[...], preferred_element_type=F32)
        a = (g * jax.nn.sigmoid(g) * u).astype(BF16)
        y = jnp.dot(a, wd_s[...], preferred_element_type=F32)
        y_s[slot] = y * rw_ref[...]

    @pl.when(jnp.logical_and(i >= 1, i - 1 < nu_ref[0]))
    def _():
        block_rows(1 - slot).wait()

    @pl.when(used)
    def _():
        for r in range(tm):
            pltpu.make_async_copy(y_s.at[slot, pl.ds(r, 1), :],
                                  o_hbm.at[pl.ds(dst_ref[0, r], 1), :], sem.at[slot]).start()

    @pl.when(jnp.logical_and(i == pl.num_programs(0) - 1, used))
    def _():
        block_rows(slot).wait()


def _moe_add_kernel(x_ref, a_ref, b_ref, g_ref, o_ref, *, normalise):
    y = x_ref[...] + a_ref[...] + b_ref[...]
    o_ref[...] = _rms(y, g_ref[...]) if normalise else y


def moe_block(x, norm_g, w_rg, b_rg, w_re, b_re, w_gate, w_up, w_down, layer, out_norm=None, tm=256):
    nb, L, d = x.shape
    T = nb * L
    ne = N_EXPERTS
    x2 = x.reshape(T, d)
    nr = N_GROUPS + ne
    w_r = jnp.concatenate([w_rg, w_re, jnp.zeros((d, LANES - nr), w_rg.dtype)], axis=1).astype(BF16)
    tr = _row_tile(T, 512)
    logits, hp = pl.pallas_call(
        _router_kernel,
        grid=(T // tr,),
        in_specs=[pl.BlockSpec((tr, d), lambda i: (i, 0)),
                  pl.BlockSpec((1, d), lambda i: (0, 0)),
                  pl.BlockSpec((d, LANES), lambda i: (0, 0))],
        out_specs=[pl.BlockSpec((tr, LANES), lambda i: (i, 0)),
                   pl.BlockSpec((tr, d // 2), lambda i: (i, 0))],
        out_shape=[jax.ShapeDtypeStruct((T, LANES), F32), jax.ShapeDtypeStruct((T, d // 2), jnp.uint32)],
        compiler_params=pltpu.CompilerParams(dimension_semantics=("parallel",)),
        name="moe_router",
    )(x2, norm_g.reshape(1, d).astype(F32), w_r)

    lg = logits[:, :N_GROUPS] + b_rg.astype(F32)
    pg = jax.nn.softmax(lg, axis=-1)
    gsel = jnp.argmax(lg, axis=-1).astype(jnp.int32)
    pg_sel = jnp.take_along_axis(pg, gsel[:, None], axis=1)
    le = (logits[:, N_GROUPS:nr] + b_re.astype(F32)).reshape(T, N_GROUPS, EXPERTS_PER_GROUP)
    le = jnp.take_along_axis(le, gsel[:, None, None], axis=1)[:, 0]
    pe = jax.nn.softmax(le, axis=-1)
    top_p, top_i = lax.top_k(pe, TOP_K)
    gate = pg_sel * top_p / jnp.sum(top_p, axis=-1, keepdims=True)
    eid = (gsel[:, None] * EXPERTS_PER_GROUP + top_i.astype(jnp.int32)).reshape(-1)
    gw = gate.reshape(-1)

    na = T * TOP_K
    order = jnp.argsort(eid).astype(jnp.int32)
    counts = jnp.bincount(eid, length=ne).astype(jnp.int32)
    start = jnp.cumsum(counts) - counts
    pcounts = (counts + tm - 1) // tm * tm
    pend = jnp.cumsum(pcounts)
    pstart = pend - pcounts
    n_rows = na + ne * tm
    n_blk = n_rows // tm
    blk_first = jnp.arange(n_blk, dtype=jnp.int32) * tm
    blk_e = jnp.minimum(jnp.sum((pend[None, :] <= blk_first[:, None]).astype(jnp.int32), axis=1), ne - 1)
    rep = lambda per_blk: jnp.repeat(per_blk, tm)
    rows = jnp.arange(n_rows, dtype=jnp.int32)
    off = rows - rep(pstart[blk_e])
    live = off < rep(counts[blk_e])
    row_a = order[jnp.clip(rep(start[blk_e]) + off, 0, na - 1)]
    row_tok = jnp.where(live, row_a // TOP_K, 0)
    row_w = jnp.where(live, gw[row_a], 0.0)
    row_dst = jnp.where(live, (row_a % TOP_K) * T + row_a // TOP_K, TOP_K * T + rows % tm)
    n_used = (pend[-1:] // tm).astype(jnp.int32)

    de = w_gate.shape[-1]
    idx_blk = lambda shift: pl.BlockSpec(
        (None, 1, tm), lambda i, be, nu: (jnp.minimum(i + shift, n_blk - 1), 0, 0), memory_space=pltpu.SMEM)
    w_in_blk = pl.BlockSpec((None, None, d, de), lambda i, be, nu: (layer, be[i], 0, 0))
    row_tok3 = row_tok.reshape(n_blk, 1, tm)
    y = pl.pallas_call(
        functools.partial(_expert_kernel, dump_row=TOP_K * T),
        grid_spec=pltpu.PrefetchScalarGridSpec(
            num_scalar_prefetch=2,
            grid=(n_blk,),
            in_specs=[idx_blk(0), idx_blk(0), idx_blk(1),
                      pl.BlockSpec(memory_space=pl.ANY),
                      w_in_blk, w_in_blk,
                      pl.BlockSpec((None, None, de, d), lambda i, be, nu: (layer, be[i], 0, 0)),
                      pl.BlockSpec((tm, 1), lambda i, be, nu: (i, 0))],
            out_specs=pl.BlockSpec(memory_space=pl.ANY),
            scratch_shapes=[pltpu.VMEM((d, de), BF16), pltpu.VMEM((d, de), BF16),
                            pltpu.VMEM((de, d), BF16), pltpu.VMEM((2, tm, d // 2), jnp.uint32),
                            pltpu.VMEM((2, tm, d), F32),
                            pltpu.SemaphoreType.DMA((2,)), pltpu.SemaphoreType.DMA((2,))]),
        out_shape=jax.ShapeDtypeStruct((TOP_K * T + tm, d), F32),
        compiler_params=pltpu.CompilerParams(
            dimension_semantics=("arbitrary",), vmem_limit_bytes=VMEM_LIMIT),
        name="moe_experts",
    )(blk_e, n_used, row_dst.reshape(n_blk, 1, tm), row_tok3, row_tok3, hp, w_gate, w_up, w_down,
      row_w[:, None])

    assert TOP_K == 2
    ta = _row_tile(T, 512)
    blk = lambda shift: pl.BlockSpec((ta, d), lambda i: (i + shift, 0))
    gain = jnp.ones((d,), F32) if out_norm is None else out_norm
    out = pl.pallas_call(
        functools.partial(_moe_add_kernel, normalise=out_norm is not None),
        grid=(T // ta,),
        in_specs=[blk(0), blk(0), blk(T // ta), pl.BlockSpec((1, d), lambda i: (0, 0))],
        out_specs=blk(0),
        out_shape=jax.ShapeDtypeStruct((T, d), F32),
        compiler_params=pltpu.CompilerParams(dimension_semantics=("parallel",)),
        name="moe_combine",
    )(x2, y, y, gain.reshape(1, d).astype(F32))
    return out.reshape(nb, L, d)


def encoder_trunk(x, norm_mix, norm_ffn, norm_final, mla_p, hy_p, moe_p):
    for i in range(DEPTH):
        j = i // 2
        if i % 2 == 0:
            x = mla_block(x, norm_mix[i], *[p[j] for p in mla_p])
        else:
            x = hyena_block(x, norm_mix[i], *[p[j] for p in hy_p])
        x = moe_block(x, norm_ffn[i], *[p[i] for p in moe_p[:4]], *moe_p[4:], layer=i,
                      out_norm=norm_final if i == DEPTH - 1 else None)
    return x


def kernel(x_prompt, x_sample, norm_mix, norm_ffn, norm_final, mla_w_in, mla_q_norm, mla_w_q_up, mla_kv_norm, mla_w_kv_up, mla_w_out, hy_w_in, hy_conv_w, hy_conv_b, hy_f_w1, hy_f_b1, hy_f_freq1, hy_f_w2, hy_f_b2, hy_f_freq2, hy_f_w3, hy_f_b3, hy_skip, hy_w_out, moe_w_group, moe_b_group, moe_w_expert, moe_b_expert, moe_w_gate, moe_w_up, moe_w_down):
    mla_p = (mla_w_in, mla_q_norm, mla_w_q_up, mla_kv_norm, mla_w_kv_up, mla_w_out)
    hy_p = (hy_w_in, hy_conv_w, hy_conv_b, hy_f_w1, hy_f_b1, hy_f_freq1, hy_f_w2, hy_f_b2,
            hy_f_freq2, hy_f_w3, hy_f_b3, hy_skip, hy_w_out)
    moe_p = (moe_w_group, moe_b_group, moe_w_expert, moe_b_expert, moe_w_gate, moe_w_up, moe_w_down)
    y_prompt = encoder_trunk(x_prompt, norm_mix, norm_ffn, norm_final, mla_p, hy_p, moe_p)
    y_sample = encoder_trunk(x_sample, norm_mix, norm_ffn, norm_final, mla_p, hy_p, moe_p)
    return (y_prompt, y_sample)
```
